```python
import functools
import jax, jax.numpy as jnp
from jax import lax
import numpy as np

D_MODEL = 2048
BATCH = 16
SEQ = 2048
DEPTH = 1
DEC_BATCH = 32
DEC_SEQ = 4
PAST_LEN = 16384
PAGE_SIZE = 128

ATT_HEADS = 8
KV_HEADS = 2
HEAD_DIM = 128
GROUP = ATT_HEADS // KV_HEADS
ROT_DIM = HEAD_DIM // 4
ROPE_THETA = 500000.0
IDX_HEADS = 16
IDX_DIM = 64
IDX_ROT_DIM = IDX_DIM // 4
IDX_SCALE = (IDX_HEADS ** -0.5) * (IDX_DIM ** -0.5)
TOPK_MAX = 256
Q_BLOCK = 128
SSM_WIDTH = 512
SSM_GROUP = 16
SSM_GROUPS = SSM_WIDTH // SSM_GROUP
SSM_STATE = 64
D_FF = 5504
CONV_W = 3
PLE_DIM = 256
EPS = 1e-6

ATT_Q_W = ATT_HEADS * HEAD_DIM
ATT_KV_W = KV_HEADS * HEAD_DIM
IDX_Q_W = IDX_HEADS * IDX_DIM
IN_SIZES = (ATT_Q_W, ATT_KV_W, ATT_KV_W, IDX_Q_W, IDX_DIM, IDX_HEADS, SSM_WIDTH, D_MODEL, D_MODEL)
IN_SPLITS = tuple(int(s) for s in np.cumsum(IN_SIZES)[:-1])
IN_WIDTH = int(sum(IN_SIZES))

kernel_name = "dsa_s5_gated_hybrid_step"

F32 = jnp.float32


def rmsnorm(x, g):
    xf = x.astype(F32)
    y = xf * lax.rsqrt(jnp.mean(xf * xf, axis=-1, keepdims=True) + EPS)
    return (y * g.astype(F32)).astype(x.dtype)


def partial_rope(x, pos, rot_dim):
    half = rot_dim // 2
    inv_freq = ROPE_THETA ** (-jnp.arange(half, dtype=F32) / half)
    ang = pos.astype(F32)[:, None] * inv_freq[None, :]
    cos = jnp.cos(ang)[:, None, :]
    sin = jnp.sin(ang)[:, None, :]
    xr = x[..., :rot_dim].astype(F32)
    x1, x2 = xr[..., :half], xr[..., half:]
    rot = jnp.concatenate([x1 * cos - x2 * sin, x2 * cos + x1 * sin], axis=-1).astype(x.dtype)
    return jnp.concatenate([rot, x[..., rot_dim:]], axis=-1)


def indexer_scores(iq, iw, ik):
    qk = jnp.einsum('bthd,bsd->bths', iq, ik, preferred_element_type=F32)
    return jnp.einsum('bth,bths->bts', iw.astype(F32), jax.nn.relu(qk))


def sparse_attend(q, k_sel, v_sel, valid):
    B, T = q.shape[0], q.shape[1]
    qg = q.reshape(B, T, KV_HEADS, GROUP, HEAD_DIM)
    s = jnp.einsum('btgrd,btkgd->btgrk', qg, k_sel, preferred_element_type=F32) * (HEAD_DIM ** -0.5)
    s = jnp.where(valid[:, :, None, None, :], s, -jnp.inf)
    p = jax.nn.softmax(s, axis=-1)
    o = jnp.einsum('btgrk,btkgd->btgrd', p.astype(v_sel.dtype), v_sel)
    return o.reshape(B, T, ATT_Q_W)


def gather_rows(src, idx):
    return jax.vmap(lambda s, i: s[i])(src, idx)


def prompt_sparse_attention(q, k, v, iq, ik, iw):
    B, T = q.shape[0], q.shape[1]
    topk = min(TOPK_MAX, T // 4)
    n_blk = T // Q_BLOCK
    key_pos = jnp.arange(T)

    def block(i):
        t0 = i * Q_BLOCK
        qb = lax.dynamic_slice_in_dim(q, t0, Q_BLOCK, axis=1)
        iqb = lax.dynamic_slice_in_dim(iq, t0, Q_BLOCK, axis=1)
        iwb = lax.dynamic_slice_in_dim(iw, t0, Q_BLOCK, axis=1)
        qpos = t0 + jnp.arange(Q_BLOCK)
        sc = indexer_scores(iqb, iwb, ik)
        sc = jnp.where((key_pos[None, :] <= qpos[:, None])[None], sc, -jnp.inf)
        _, idx = lax.top_k(sc, topk)
        valid = idx <= qpos[None, :, None]
        return sparse_attend(qb, gather_rows(k, idx), gather_rows(v, idx), valid)

    out = lax.map(block, jnp.arange(n_blk))
    return jnp.transpose(out, (1, 0, 2, 3)).reshape(B, T, ATT_Q_W)


def sample_sparse_attention(q, k, v, iq, ik, iw, cache_k, cache_v, cache_ik, page_table):
    B, T = q.shape[0], q.shape[1]
    past = page_table.shape[1] * PAGE_SIZE
    L = past + T
    topk = min(TOPK_MAX, L // 4)
    ik_past = cache_ik[page_table].reshape(B, past, IDX_DIM)
    ik_all = jnp.concatenate([ik_past, ik.astype(ik_past.dtype)], axis=1)
    qpos = past + jnp.arange(T)
    key_pos = jnp.arange(L)
    sc = indexer_scores(iq, iw, ik_all)
    sc = jnp.where((key_pos[None, :] <= qpos[:, None])[None], sc, -jnp.inf)
    _, idx = lax.top_k(sc, topk)
    valid = idx <= qpos[None, :, None]
    in_past = idx < past
    pidx = jnp.minimum(idx, past - 1)
    phys = jnp.take_along_axis(page_table, (pidx // PAGE_SIZE).reshape(B, -1), axis=1).reshape(pidx.shape)
    rows = phys * PAGE_SIZE + pidx % PAGE_SIZE
    flat_k = cache_k.reshape(-1, KV_HEADS, HEAD_DIM)
    flat_v = cache_v.reshape(-1, KV_HEADS, HEAD_DIM)
    nidx = jnp.clip(idx - past, 0, T - 1)
    sel = in_past[..., None, None]
    k_sel = jnp.where(sel, flat_k[rows], gather_rows(k, nidx).astype(flat_k.dtype))
    v_sel = jnp.where(sel, flat_v[rows], gather_rows(v, nidx).astype(flat_v.dtype))
    return sparse_attend(q.astype(k_sel.dtype), k_sel, v_sel, valid).astype(q.dtype)


def complex_affine_combine(e1, e2):
    a1r, a1i, b1r, b1i = e1
    a2r, a2i, b2r, b2i = e2
    ar = a2r * a1r - a2i * a1i
    ai = a2r * a1i + a2i * a1r
    br = a2r * b1r - a2i * b1i + b2r
    bi = a2r * b1i + a2i * b1r + b2i
    return (ar, ai, br, bi)


def s5_branch(u, s0_re, s0_im, lw):
    B, T = u.shape[0], u.shape[1]
    ug = u.reshape(B, T, SSM_GROUPS, SSM_GROUP).astype(F32)
    a_re = lw['ssm_a_re'].astype(F32)
    a_im = lw['ssm_a_im'].astype(F32)
    dt = jnp.exp(lw['ssm_log_dt'].astype(F32))[:, None]
    mag = jnp.exp(a_re * dt)
    ang = a_im * dt
    ab_re, ab_im = mag * jnp.cos(ang), mag * jnp.sin(ang)
    den = a_re * a_re + a_im * a_im
    f_re = ((ab_re - 1.0) * a_re + ab_im * a_im) / den
    f_im = (ab_im * a_re - (ab_re - 1.0) * a_im) / den
    b_re = lw['ssm_b_re'].astype(F32)
    b_im = lw['ssm_b_im'].astype(F32)
    bb_re = f_re[..., None] * b_re - f_im[..., None] * b_im
    bb_im = f_re[..., None] * b_im + f_im[..., None] * b_re
    bu_re = jnp.einsum('btgc,gnc->btgn', ug, bb_re)
    bu_im = jnp.einsum('btgc,gnc->btgn', ug, bb_im)
    a_seq_re = jnp.broadcast_to(ab_re[None, None], (1, T, SSM_GROUPS, SSM_STATE))
    a_seq_im = jnp.broadcast_to(ab_im[None, None], (1, T, SSM_GROUPS, SSM_STATE))
    acum_re, acum_im, s_re, s_im = lax.associative_scan(
        complex_affine_combine, (a_seq_re, a_seq_im, bu_re, bu_im), axis=1)
    x0r = s0_re.astype(F32)[:, None]
    x0i = s0_im.astype(F32)[:, None]
    s_re = s_re + acum_re * x0r - acum_im * x0i
    s_im = s_im + acum_re * x0i + acum_im * x0r
    y = (jnp.einsum('btgn,gcn->btgc', s_re, lw['ssm_c_re'].astype(F32))
         - jnp.einsum('btgn,gcn->btgc', s_im, lw['ssm_c_im'].astype(F32))
         + ug * lw['ssm_d'].astype(F32).reshape(SSM_GROUPS, SSM_GROUP))
    y = y.reshape(B, T, SSM_WIDTH).astype(u.dtype)
    z = jax.nn.gelu(y, approximate=True)
    z = z * jax.nn.sigmoid(z @ lw['w_glu'])
    return z, s_re[:, -1], s_im[:, -1]


def conv_ffn(h, conv0, lw):
    T = h.shape[1]
    up = h @ lw['w_up']
    ext = jnp.concatenate([conv0.astype(up.dtype), up], axis=1)
    cw = lw['conv_w']
    c = sum(ext[:, j:j + T] * cw[j] for j in range(CONV_W)) + lw['conv_b']
    gate, val = jnp.split(c, 2, axis=-1)
    out = (jax.nn.gelu(gate, approximate=True) * val) @ lw['w_down']
    return out, ext[:, -(CONV_W - 1):]


def decoder_layer(x, p, pos, attend, s0_re, s0_im, conv0, lw):
    B, T, _ = x.shape
    h = rmsnorm(x, lw['g_mix'])
    proj = h @ lw['w_in']
    q, k, v, iq, ik, iw, u, ga, gs = jnp.split(proj, IN_SPLITS, axis=-1)
    q = partial_rope(rmsnorm(q.reshape(B, T, ATT_HEADS, HEAD_DIM), lw['g_q']), pos, ROT_DIM)
    k = partial_rope(rmsnorm(k.reshape(B, T, KV_HEADS, HEAD_DIM), lw['g_k']), pos, ROT_DIM)
    v = v.reshape(B, T, KV_HEADS, HEAD_DIM)
    iq = partial_rope(iq.reshape(B, T, IDX_HEADS, IDX_DIM), pos, IDX_ROT_DIM)
    ik = partial_rope(ik.reshape(B, T, 1, IDX_DIM), pos, IDX_ROT_DIM).reshape(B, T, IDX_DIM)
    iw = iw * IDX_SCALE
    o_att = attend(q, k, v, iq, ik, iw)
    z, s_re, s_im = s5_branch(u, s0_re, s0_im, lw)
    merged = (jax.nn.sigmoid(ga) * (o_att @ lw['w_att_br'])
              + jax.nn.sigmoid(gs) * (z @ lw['w_ssm_br']))
    x = x + merged @ lw['w_o']
    f, conv_new = conv_ffn(rmsnorm(x, lw['g_ffn']), conv0, lw)
    x = x + f
    ple = rmsnorm(p @ lw['w_ple'], lw['g_ple'])
    x = x + jax.nn.sigmoid(rmsnorm(x, lw['g_pg']) @ lw['w_pg']) * ple
    return x, k, v, ik, s_re, s_im, conv_new


def setup_inputs(seed: int = 0) -> dict:
    key = jax.random.key(seed)
    ks = iter(jax.random.split(key, 48))

    def nrm(shape, scale):
        return jax.random.normal(next(ks), shape, F32) * scale

    n_pages = PAST_LEN // PAGE_SIZE
    n_pool = (DEC_BATCH * n_pages * 5) // 4
    page_table = jax.random.permutation(next(ks), n_pool)[:DEC_BATCH * n_pages]
    page_table = page_table.reshape(DEC_BATCH, n_pages).astype(jnp.int32)
    log_dt = jax.random.uniform(next(ks), (DEPTH, SSM_GROUPS), F32, float(np.log(1e-3)), float(np.log(1e-1)))
    a_im = jnp.pi * jnp.arange(SSM_STATE, dtype=F32)[None, None, :] + nrm((DEPTH, SSM_GROUPS, SSM_STATE), 0.01)
    return {
        'x_prompt': nrm((BATCH, SEQ, D_MODEL), 1.0),
        'x_sample': nrm((DEC_BATCH, DEC_SEQ, D_MODEL), 1.0),
        'cache_k': nrm((DEPTH, n_pool, PAGE_SIZE, KV_HEADS, HEAD_DIM), 1.0),
        'cache_v': nrm((DEPTH, n_pool, PAGE_SIZE, KV_HEADS, HEAD_DIM), 1.0),
        'cache_idx_k': nrm((DEPTH, n_pool, PAGE_SIZE, IDX_DIM), 1.0),
        'state_ssm_re': nrm((DEPTH, DEC_BATCH, SSM_GROUPS, SSM_STATE), 0.1),
        'state_ssm_im': nrm((DEPTH, DEC_BATCH, SSM_GROUPS, SSM_STATE), 0.1),
        'state_conv': nrm((DEPTH, DEC_BATCH, CONV_W - 1, 2 * D_FF), 0.5),
        'page_table': page_table,
        'p_prompt': nrm((DEPTH, BATCH, SEQ, PLE_DIM), 1.0),
        'p_sample': nrm((DEPTH, DEC_BATCH, DEC_SEQ, PLE_DIM), 1.0),
        'g_mix': 1.0 + nrm((DEPTH, D_MODEL), 0.05),
        'w_in': nrm((DEPTH, D_MODEL, IN_WIDTH), D_MODEL ** -0.5),
        'g_q': 1.0 + nrm((DEPTH, HEAD_DIM), 0.05),
        'g_k': 1.0 + nrm((DEPTH, HEAD_DIM), 0.05),
        'ssm_a_re': -0.5 + nrm((DEPTH, SSM_GROUPS, SSM_STATE), 0.01),
        'ssm_a_im': a_im,
        'ssm_log_dt': log_dt,
        'ssm_b_re': nrm((DEPTH, SSM_GROUPS, SSM_STATE, SSM_GROUP), (2 * SSM_GROUP) ** -0.5),
        'ssm_b_im': nrm((DEPTH, SSM_GROUPS, SSM_STATE, SSM_GROUP), (2 * SSM_GROUP) ** -0.5),
        'ssm_c_re': nrm((DEPTH, SSM_GROUPS, SSM_GROUP, SSM_STATE), (2 * SSM_STATE) ** -0.5),
        'ssm_c_im': nrm((DEPTH, SSM_GROUPS, SSM_GROUP, SSM_STATE), (2 * SSM_STATE) ** -0.5),
        'ssm_d': nrm((DEPTH, SSM_WIDTH), 1.0),
        'w_glu': nrm((DEPTH, SSM_WIDTH, SSM_WIDTH), SSM_WIDTH ** -0.5),
        'w_att_br': nrm((DEPTH, ATT_Q_W, D_MODEL), ATT_Q_W ** -0.5),
        'w_ssm_br': nrm((DEPTH, SSM_WIDTH, D_MODEL), SSM_WIDTH ** -0.5),
        'w_o': nrm((DEPTH, D_MODEL, D_MODEL), D_MODEL ** -0.5),
        'g_ffn': 1.0 + nrm((DEPTH, D_MODEL), 0.05),
        'w_up': nrm((DEPTH, D_MODEL, 2 * D_FF), D_MODEL ** -0.5),
        'conv_w': nrm((DEPTH, CONV_W, 2 * D_FF), CONV_W ** -0.5),
        'conv_b': nrm((DEPTH, 2 * D_FF), 0.02),
        'w_down': nrm((DEPTH, D_FF, D_MODEL), D_FF ** -0.5),
        'w_ple': nrm((DEPTH, PLE_DIM, D_MODEL), PLE_DIM ** -0.5),
        'g_ple': 1.0 + nrm((DEPTH, D_MODEL), 0.05),
        'g_pg': 1.0 + nrm((DEPTH, D_MODEL), 0.05),
        'w_pg': nrm((DEPTH, D_MODEL, D_MODEL), D_MODEL ** -0.5),
    }


def reference(x_prompt, x_sample, cache_k, cache_v, cache_idx_k, state_ssm_re, state_ssm_im, state_conv,
              page_table, p_prompt, p_sample, g_mix, w_in, g_q, g_k, ssm_a_re, ssm_a_im, ssm_log_dt,
              ssm_b_re, ssm_b_im, ssm_c_re, ssm_c_im, ssm_d, w_glu, w_att_br, w_ssm_br, w_o, g_ffn,
              w_up, conv_w, conv_b, w_down, w_ple, g_ple, g_pg, w_pg):
    Bp, Tp = x_prompt.shape[0], x_prompt.shape[1]
    Ts = x_sample.shape[1]
    past = page_table.shape[1] * PAGE_SIZE
    pos_prompt = jnp.arange(Tp)
    pos_sample = past + jnp.arange(Ts)
    xp, xs = x_prompt, x_sample
    kp, vp, ikp, srp, sip, cvp = [], [], [], [], [], []
    ks_, vs_, iks, srs, sis, cvs = [], [], [], [], [], []
    for i in range(DEPTH):
        lw = dict(g_mix=g_mix[i], w_in=w_in[i], g_q=g_q[i], g_k=g_k[i],
                  ssm_a_re=ssm_a_re[i], ssm_a_im=ssm_a_im[i], ssm_log_dt=ssm_log_dt[i],
                  ssm_b_re=ssm_b_re[i], ssm_b_im=ssm_b_im[i], ssm_c_re=ssm_c_re[i], ssm_c_im=ssm_c_im[i],
                  ssm_d=ssm_d[i], w_glu=w_glu[i], w_att_br=w_att_br[i], w_ssm_br=w_ssm_br[i], w_o=w_o[i],
                  g_ffn=g_ffn[i], w_up=w_up[i], conv_w=conv_w[i], conv_b=conv_b[i], w_down=w_down[i],
                  w_ple=w_ple[i], g_ple=g_ple[i], g_pg=g_pg[i], w_pg=w_pg[i])
        zs = jnp.zeros((Bp, SSM_GROUPS, SSM_STATE), F32)
        zc = jnp.zeros((Bp, CONV_W - 1, 2 * D_FF), xp.dtype)
        xp, k1, v1, ik1, sr1, si1, cv1 = decoder_layer(
            xp, p_prompt[i], pos_prompt, prompt_sparse_attention, zs, zs, zc, lw)
        kp.append(k1); vp.append(v1); ikp.append(ik1); srp.append(sr1); sip.append(si1); cvp.append(cv1)
        attend_s = functools.partial(sample_sparse_attention, cache_k=cache_k[i], cache_v=cache_v[i],
                                     cache_ik=cache_idx_k[i], page_table=page_table)
        xs, k2, v2, ik2, sr2, si2, cv2 = decoder_layer(
            xs, p_sample[i], pos_sample, attend_s, state_ssm_re[i], state_ssm_im[i], state_conv[i], lw)
        ks_.append(k2); vs_.append(v2); iks.append(ik2); srs.append(sr2); sis.append(si2); cvs.append(cv2)
    return (xp, xs,
            jnp.stack(kp, 0), jnp.stack(vp, 0), jnp.stack(ikp, 0),
            jnp.stack(srp, 0), jnp.stack(sip, 0), jnp.stack(cvp, 0),
            jnp.stack(ks_, 0), jnp.stack(vs_, 0), jnp.stack(iks, 0),
            jnp.stack(srs, 0), jnp.stack(sis, 0), jnp.stack(cvs, 0))
```

```python
import functools

import numpy as np
import jax
import jax.numpy as jnp
from jax import lax
from jax.experimental import pallas as pl
from jax.experimental.pallas import tpu as pltpu

F32 = jnp.float32
BF16 = jnp.bfloat16
I32 = jnp.int32

ATT_HEADS = 8
KV_HEADS = 2
HEAD_DIM = 128
GROUP = ATT_HEADS // KV_HEADS
ROT_DIM = HEAD_DIM // 4
ROPE_THETA = 500000.0
IDX_HEADS = 16
IDX_DIM = 64
IDX_ROT_DIM = IDX_DIM // 4
IDX_SCALE = (IDX_HEADS ** -0.5) * (IDX_DIM ** -0.5)
TOPK_MAX = 256
Q_BLOCK = 128
PAGE_SIZE = 128
SSM_GROUP = 16
SSM_STATE = 64
CONV_W = 3
EPS = 1e-6

LANES = 128
SUBLANES = 8
NEG = -1e30
INT_MIN = -2 ** 31
VMEM_LIMIT = 56 * 1024 * 1024

ATT_Q_W = ATT_HEADS * HEAD_DIM
ATT_KV_W = KV_HEADS * HEAD_DIM
IDX_Q_W = IDX_HEADS * IDX_DIM


def _cparams(*sem):
    return pltpu.CompilerParams(dimension_semantics=sem, vmem_limit_bytes=VMEM_LIMIT)


def _const_spec(shape):
    nd = len(shape)
    return pl.BlockSpec(shape, lambda *_: (0,) * nd, pipeline_mode=pl.Buffered(1))


def _rms(x, g):
    ms = jnp.mean(x * x, axis=-1, keepdims=True)
    return (x * lax.rsqrt(ms + EPS)) * g


def _gelu_tanh(x):
    return 0.5 * x * (1.0 + jnp.tanh(0.7978845608028654 * (x + 0.044715 * (x * x * x))))


def _sigmoid(x):
    return 1.0 / (1.0 + jnp.exp(-x))


def _rope(x, tab_ref, half):
    return (x * tab_ref[0] + pltpu.roll(x, half, 1) * tab_ref[1]
            + pltpu.roll(x, LANES - half, 1) * tab_ref[2])


def _dot_t(a, b):
    return lax.dot_general(a, b, (((1,), (1,)), ((), ())), preferred_element_type=F32)


def _qkv_kernel(x_ref, g_ref, w_ref, gq_ref, gk_ref, tab_ref, q_ref, k_ref, v_ref, kb_ref, vb_ref):
    h = _rms(x_ref[...], g_ref[...]).astype(BF16)
    y = jnp.dot(h, w_ref[...], preferred_element_type=F32)
    half = ROT_DIM // 2
    for hh in range(ATT_HEADS):
        sl = slice(hh * HEAD_DIM, (hh + 1) * HEAD_DIM)
        qh = _rope(_rms(y[:, sl], gq_ref[...]), tab_ref, half)
        q_ref[:, sl] = (qh * (HEAD_DIM ** -0.5)).astype(BF16)
    for hh in range(KV_HEADS):
        sl = slice(hh * HEAD_DIM, (hh + 1) * HEAD_DIM)
        kh = _rope(_rms(y[:, ATT_Q_W + hh * HEAD_DIM:ATT_Q_W + (hh + 1) * HEAD_DIM], gk_ref[...]),
                   tab_ref, half)
        k_ref[:, sl] = kh
        kb_ref[:, sl] = kh.astype(BF16)
    vv = y[:, ATT_Q_W + ATT_KV_W:]
    v_ref[...] = vv
    vb_ref[...] = vv.astype(BF16)


def _qkv_proj(x, g, w, gq, gk, tab, tm):
    n, d = x.shape
    wd = w.shape[1]
    p_blocks = tab.shape[1] // tm
    row = lambda i: (i, 0)
    return pl.pallas_call(
        _qkv_kernel,
        grid=(n // tm,),
        in_specs=[pl.BlockSpec((tm, d), row), _const_spec((1, d)), _const_spec((d, wd)),
                  _const_spec((1, HEAD_DIM)), _const_spec((1, HEAD_DIM)),
                  pl.BlockSpec((3, tm, LANES), lambda i: (0, i % p_blocks, 0))],
        out_specs=[pl.BlockSpec((tm, ATT_Q_W), row), pl.BlockSpec((tm, ATT_KV_W), row),
                   pl.BlockSpec((tm, ATT_KV_W), row), pl.BlockSpec((tm, ATT_KV_W), row),
                   pl.BlockSpec((tm, ATT_KV_W), row)],
        out_shape=[jax.ShapeDtypeStruct((n, ATT_Q_W), BF16), jax.ShapeDtypeStruct((n, ATT_KV_W), F32),
                   jax.ShapeDtypeStruct((n, ATT_KV_W), F32), jax.ShapeDtypeStruct((n, ATT_KV_W), BF16),
                   jax.ShapeDtypeStruct((n, ATT_KV_W), BF16)],
        compiler_params=_cparams("arbitrary"),
        name="qkv_proj",
    )(x, g, w, gq, gk, tab)


def _idx_kernel(x_ref, g_ref, w_ref, tab_ref, iq_ref, ik_ref, ikb_ref, iw_ref):
    h = _rms(x_ref[...], g_ref[...]).astype(BF16)
    y = jnp.dot(h, w_ref[...], preferred_element_type=F32)
    half = IDX_ROT_DIM // 2
    for s in range(IDX_Q_W // LANES):
        sl = slice(s * LANES, (s + 1) * LANES)
        iq_ref[:, sl] = _rope(y[:, sl], tab_ref, half).astype(BF16)
    ik2 = _rope(y[:, IDX_Q_W:IDX_Q_W + LANES], tab_ref, half)
    ik_ref[...] = ik2
    ikb_ref[...] = ik2.astype(BF16)
    iw_ref[...] = y[:, IDX_Q_W + LANES:] * IDX_SCALE


def _idx_proj(x, g, w, tab, tm):
    n, d = x.shape
    wd = w.shape[1]
    p_blocks = tab.shape[1] // tm
    row = lambda i: (i, 0)
    return pl.pallas_call(
        _idx_kernel,
        grid=(n // tm,),
        in_specs=[pl.BlockSpec((tm, d), row), _const_spec((1, d)), _const_spec((d, wd)),
                  pl.BlockSpec((3, tm, LANES), lambda i: (0, i % p_blocks, 0))],
        out_specs=[pl.BlockSpec((tm, IDX_Q_W), row), pl.BlockSpec((tm, LANES), row),
                   pl.BlockSpec((tm, LANES), row), pl.BlockSpec((tm, LANES), row)],
        out_shape=[jax.ShapeDtypeStruct((n, IDX_Q_W), BF16), jax.ShapeDtypeStruct((n, LANES), F32),
                   jax.ShapeDtypeStruct((n, LANES), BF16), jax.ShapeDtypeStruct((n, LANES), F32)],
        compiler_params=_cparams("arbitrary"),
        name="idx_proj",
    )(x, g, w, tab)


def _norm_mm_kernel(x_ref, g_ref, w_ref, o_ref, h_ref):
    @pl.when(pl.program_id(1) == 0)
    def _():
        h_ref[...] = _rms(x_ref[...], g_ref[...]).astype(BF16)
    o_ref[...] = jnp.dot(h_ref[...], w_ref[...], preferred_element_type=F32).astype(o_ref.dtype)


def _norm_matmul(x, g, w, tm, tn, out_shape, out_map, name):
    n, d = x.shape
    return pl.pallas_call(
        _norm_mm_kernel,
        grid=(n // tm, w.shape[1] // tn),
        in_specs=[pl.BlockSpec((tm, d), lambda i, j: (i, 0)), _const_spec((1, d)),
                  pl.BlockSpec((d, tn), lambda i, j: (0, j))],
        out_specs=pl.BlockSpec((tm, tn), out_map),
        out_shape=out_shape,
        scratch_shapes=[pltpu.VMEM((tm, d), BF16)],
        compiler_params=_cparams("arbitrary", "arbitrary"),
        name=name,
    )(x, g, w)


def _sort_key(sc):
    bits = pltpu.bitcast(sc, I32)
    return bits ^ ((bits >> 31) & jnp.int32(0x7FFFFFFF))


def _kth_largest(count_ge, shape, topk):
    def body(it, t):
        cand = t + lax.shift_left(jnp.int32(1), 31 - it)
        return jnp.where(count_ge(cand) >= topk, cand, t)
    t = lax.fori_loop(0, 32, body, jnp.full(shape, INT_MIN, I32))
    return jnp.maximum(t, INT_MIN + 1)


def _row_groups(op, x):
    parts = [x[r:r + SUBLANES, :] for r in range(0, x.shape[0], SUBLANES)]
    while len(parts) > 1:
        nxt = [op(parts[j], parts[j + 1]) for j in range(0, len(parts) - 1, 2)]
        if len(parts) % 2:
            nxt.append(parts[-1])
        parts = nxt
    return parts[0]


def _indexer_operands(iq, iw, lhs_ref, wb_ref, rows):
    lane = lax.broadcasted_iota(I32, (rows, LANES), 1)
    for s in range(IDX_Q_W // LANES):
        x = iq[:, s * LANES:(s + 1) * LANES].astype(F32)
        lhs_ref[(2 * s) * rows:(2 * s + 1) * rows, :] = jnp.where(lane < IDX_DIM, x, 0.0).astype(lhs_ref.dtype)
        lhs_ref[(2 * s + 1) * rows:(2 * s + 2) * rows, :] = jnp.where(lane >= IDX_DIM, x, 0.0).astype(lhs_ref.dtype)
    for h in range(IDX_HEADS):
        wb_ref[h * rows:(h + 1) * rows, :] = jnp.broadcast_to(iw[:, h:h + 1], (rows, LANES))


def _indexer_scores(lhs_ref, wb_ref, keys2, rows):
    n_keys = keys2.shape[0]
    r = _dot_t(lhs_ref[...].astype(BF16), keys2)
    sc = jnp.zeros((rows, n_keys), F32)
    for h in range(IDX_HEADS):
        wh = jnp.concatenate([wb_ref[h * rows:(h + 1) * rows, :]] * (n_keys // LANES), axis=1)
        sc = sc + wh * jnp.maximum(r[h * rows:(h + 1) * rows, :], 0.0)
    return sc


def _flash_update(s, vg, m_ref, l_ref, acc_ref, idx):
    m_old = m_ref[idx]
    m_new = jnp.maximum(m_old, jnp.max(s, axis=1, keepdims=True))
    alpha = jnp.exp(m_old - m_new)
    p = jnp.exp(s - m_new)
    l_ref[idx] = alpha * l_ref[idx] + jnp.sum(p, axis=1, keepdims=True)
    acc_ref[idx] = alpha * acc_ref[idx] + jnp.dot(p.astype(BF16), vg, preferred_element_type=F32)
    m_ref[idx] = m_new


def _prompt_attn_kernel(q_ref, iq_ref, iw_ref, ikb_ref, kb_ref, vb_ref, o_ref,
                        lhs_ref, q2_ref, vt_ref, key_ref, s_ref, m_ref, l_ref, acc_ref, *, seq, kc, topk):
    qb = Q_BLOCK
    npair = ATT_HEADS // 2
    i = pl.program_id(1)
    t0 = i * qb
    nch = (t0 + qb + kc - 1) // kc

    @pl.when(i == 0)
    def _():
        def transpose_chunk(c, carry):
            c0 = pl.multiple_of(c * kc, kc)
            vt_ref[c] = vb_ref[pl.ds(c0, kc), :].astype(F32).T.astype(BF16)
            return carry
        lax.fori_loop(0, seq // kc, transpose_chunk, 0)

    lane = lax.broadcasted_iota(I32, (qb, LANES), 1)
    for s in range(IDX_Q_W // LANES):
        x = iq_ref[:, s * LANES:(s + 1) * LANES].astype(F32)
        lhs_ref[(2 * s) * qb:(2 * s + 1) * qb, :] = jnp.where(lane < IDX_DIM, x, 0.0).astype(BF16)
        lhs_ref[(2 * s + 1) * qb:(2 * s + 2) * qb, :] = jnp.where(lane >= IDX_DIM, x, 0.0).astype(BF16)
    iwt = iw_ref[...].T
    for p in range(npair):
        for r in range(2):
            hsl = slice((2 * p + r) * HEAD_DIM, (2 * p + r + 1) * HEAD_DIM)
            q2_ref[p, r * qb:(r + 1) * qb, :] = q_ref[:, hsl]

    key_pos = lax.broadcasted_iota(I32, (kc, LANES), 0)
    q_pos = t0 + lax.broadcasted_iota(I32, (kc, LANES), 1)

    def score_chunk(c, carry):
        c0 = pl.multiple_of(c * kc, kc)
        ikc = ikb_ref[pl.ds(c0, kc), :]
        sc = jnp.zeros((kc, LANES), F32)
        for s in range(IDX_HEADS // 2):
            r = _dot_t(ikc, lhs_ref[2 * s * qb:(2 * s + 2) * qb, :])
            sc = (sc + iwt[2 * s:2 * s + 1, :] * jnp.maximum(r[:, :LANES], 0.0)
                  + iwt[2 * s + 1:2 * s + 2, :] * jnp.maximum(r[:, LANES:], 0.0))
        key_ref[c] = jnp.where(c0 + key_pos <= q_pos, _sort_key(sc), INT_MIN)
        return carry
    lax.fori_loop(0, nch, score_chunk, 0)

    def count_ge(cand):
        def chunk(c, acc):
            return acc + _row_groups(jnp.add, jnp.where(key_ref[c] >= cand, 1.0, 0.0))
        acc = lax.fori_loop(0, nch, chunk, jnp.zeros((SUBLANES, LANES), F32))
        return jnp.sum(acc, axis=0, keepdims=True)
    thr = _kth_largest(count_ge, (1, LANES), float(topk))

    m_ref[...] = jnp.full(m_ref.shape, NEG, F32)
    l_ref[...] = jnp.zeros(l_ref.shape, F32)
    acc_ref[...] = jnp.zeros(acc_ref.shape, F32)

    def score_pass(c, carry):
        c0 = pl.multiple_of(c * kc, kc)
        sel = key_ref[c] >= thr
        for p in range(npair):
            gsl = slice((2 * p) // GROUP * HEAD_DIM, ((2 * p) // GROUP + 1) * HEAD_DIM)
            s = _dot_t(kb_ref[pl.ds(c0, kc), gsl], q2_ref[p])
            s = jnp.concatenate([jnp.where(sel, s[:, :LANES], NEG), jnp.where(sel, s[:, LANES:], NEG)], axis=1)
            s_ref[c, p] = s
            m_ref[p] = jnp.maximum(m_ref[p], _row_groups(jnp.maximum, s))
        return carry
    lax.fori_loop(0, nch, score_pass, 0)
    m = [jnp.max(m_ref[p], axis=0, keepdims=True) for p in range(npair)]

    def value_pass(c, carry):
        for p in range(npair):
            gsl = slice((2 * p) // GROUP * HEAD_DIM, ((2 * p) // GROUP + 1) * HEAD_DIM)
            pr = jnp.exp(s_ref[c, p] - m[p])
            l_ref[p] += _row_groups(jnp.add, pr)
            acc_ref[p] += jnp.dot(vt_ref[c, gsl, :], pr.astype(BF16),
                                  preferred_element_type=F32)
        return carry
    lax.fori_loop(0, nch, value_pass, 0)

    for p in range(npair):
        o_t = acc_ref[p] / jnp.sum(l_ref[p], axis=0, keepdims=True)
        for r in range(2):
            hsl = slice((2 * p + r) * HEAD_DIM, (2 * p + r + 1) * HEAD_DIM)
            o_ref[:, hsl] = o_t[:, r * qb:(r + 1) * qb].T.astype(BF16)


def _prompt_attention(q, iq, iw, ikb, kb, vb, batch, seq):
    kc = min(256, seq)
    topk = min(TOPK_MAX, seq // 4)
    nq = seq // Q_BLOCK
    blk = lambda b, i: (b * nq + i, 0)
    full = lambda b, i: (b, 0)
    kern = functools.partial(_prompt_attn_kernel, seq=seq, kc=kc, topk=topk)
    return pl.pallas_call(
        kern,
        grid=(batch, nq),
        in_specs=[pl.BlockSpec((Q_BLOCK, ATT_Q_W), blk), pl.BlockSpec((Q_BLOCK, IDX_Q_W), blk),
                  pl.BlockSpec((Q_BLOCK, LANES), blk), pl.BlockSpec((seq, LANES), full),
                  pl.BlockSpec((seq, ATT_KV_W), full), pl.BlockSpec((seq, ATT_KV_W), full)],
        out_specs=pl.BlockSpec((Q_BLOCK, ATT_Q_W), blk),
        out_shape=jax.ShapeDtypeStruct((batch * seq, ATT_Q_W), BF16),
        scratch_shapes=[pltpu.VMEM((IDX_HEADS * Q_BLOCK, LANES), BF16),
                        pltpu.VMEM((ATT_HEADS // 2, 2 * Q_BLOCK, HEAD_DIM), BF16),
                        pltpu.VMEM((seq // kc, ATT_KV_W, kc), BF16),
                        pltpu.VMEM((seq // kc, kc, Q_BLOCK), I32),
                        pltpu.VMEM((seq // kc, ATT_HEADS // 2, kc, 2 * Q_BLOCK), F32),
                        pltpu.VMEM((ATT_HEADS // 2, SUBLANES, 2 * Q_BLOCK), F32),
                        pltpu.VMEM((ATT_HEADS // 2, SUBLANES, 2 * Q_BLOCK), F32),
                        pltpu.VMEM((ATT_HEADS // 2, HEAD_DIM, 2 * Q_BLOCK), F32)],
        compiler_params=_cparams("arbitrary", "arbitrary"),
        name="prompt_attention",
    )(q, iq, iw, ikb, kb, vb)


def _sample_score_kernel(pt_ref, iq_ref, iw_ref, ikn_ref, *rest, pages, n_new):
    page_refs = rest[:pages]
    sc_ref, thr_ref, scn_ref, lhs_ref, wb_ref, key_ref = rest[pages:]
    del pt_ref
    c = pl.program_id(1)
    nchunks = pl.num_programs(1)
    rows = SUBLANES

    @pl.when(c == 0)
    def _():
        _indexer_operands(iq_ref[0], iw_ref[0], lhs_ref, wb_ref, rows)

    keys2 = jnp.concatenate(
        [jnp.concatenate([page_refs[p][0], page_refs[p][0]], axis=1) for p in range(pages)],
        axis=0).astype(BF16)
    sc = _indexer_scores(lhs_ref, wb_ref, keys2, rows)
    sc_ref[0] = sc
    key = _sort_key(sc)
    for p in range(pages):
        key_ref[c * pages + p] = key[:, p * PAGE_SIZE:(p + 1) * PAGE_SIZE]

    @pl.when(c == nchunks - 1)
    def _():
        scn = _indexer_scores(lhs_ref, wb_ref, ikn_ref[0], rows)
        scn_ref[0] = scn
        t = lax.broadcasted_iota(I32, (rows, LANES), 0)
        j = lax.broadcasted_iota(I32, (rows, LANES), 1)
        keyn = jnp.where(j <= t, _sort_key(scn), INT_MIN)
        total = key_ref.shape[0]

        def count_ge(cand):
            hit = jnp.sum(jnp.where(key_ref[...] >= cand[None], 1.0, 0.0), axis=0)
            return jnp.sum(hit + jnp.where(keyn >= cand, 1.0, 0.0), axis=1, keepdims=True)
        topk = float(min(TOPK_MAX, (total * PAGE_SIZE + n_new) // 4))
        thr = _kth_largest(count_ge, (rows, 1), topk)
        thr_ref[0] = jnp.broadcast_to(thr, (rows, LANES))


def _sample_scores(page_table, iq, iw, ikb_new, cache_ik, pages, n_new):
    batch, n_pages = page_table.shape
    rows = SUBLANES
    past = n_pages * PAGE_SIZE
    nchunks = n_pages // pages
    seq = lambda b, c, pt: (b, 0, 0)
    page_specs = [pl.BlockSpec((1, PAGE_SIZE, IDX_DIM),
                               functools.partial(lambda b, c, pt, p: (pt[b, c * pages + p], 0, 0), p=p))
                  for p in range(pages)]
    kern = functools.partial(_sample_score_kernel, pages=pages, n_new=n_new)
    grid_spec = pltpu.PrefetchScalarGridSpec(
        num_scalar_prefetch=1,
        grid=(batch, nchunks),
        in_specs=[pl.BlockSpec((1, rows, IDX_Q_W), seq), pl.BlockSpec((1, rows, LANES), seq),
                  pl.BlockSpec((1, LANES, LANES), seq)] + page_specs,
        out_specs=[pl.BlockSpec((1, rows, pages * PAGE_SIZE), lambda b, c, pt: (b, 0, c)),
                   pl.BlockSpec((1, rows, LANES), seq), pl.BlockSpec((1, rows, LANES), seq)],
        scratch_shapes=[pltpu.VMEM((IDX_HEADS * rows, LANES), F32),
                        pltpu.VMEM((IDX_HEADS * rows, LANES), F32),
                        pltpu.VMEM((n_pages, rows, PAGE_SIZE), I32)],
    )
    return pl.pallas_call(
        kern,
        grid_spec=grid_spec,
        out_shape=[jax.ShapeDtypeStruct((batch, rows, past), F32),
                   jax.ShapeDtypeStruct((batch, rows, LANES), I32),
                   jax.ShapeDtypeStruct((batch, rows, LANES), F32)],
        compiler_params=_cparams("arbitrary", "arbitrary"),
        name="sample_scores",
    )(page_table, iq, iw, ikb_new, *([cache_ik] * pages))


def _sample_attn_kernel(pt_ref, q_ref, sc_ref, thr_ref, scn_ref, kn_ref, vn_ref, *rest, pages):
    k_refs = rest[:pages]
    v_refs = rest[pages:2 * pages]
    o_ref, m_ref, l_ref, acc_ref = rest[2 * pages:]
    del pt_ref
    c = pl.program_id(1)
    nchunks = pl.num_programs(1)
    rows = SUBLANES
    thr = thr_ref[0][:, 0:1]

    @pl.when(c == 0)
    def _():
        m_ref[...] = jnp.full(m_ref.shape, NEG, F32)
        l_ref[...] = jnp.zeros(l_ref.shape, F32)
        acc_ref[...] = jnp.zeros(acc_ref.shape, F32)

    def update(g, sel, kg, vg):
        qg = jnp.concatenate(
            [q_ref[0, :, (g * GROUP + r) * HEAD_DIM:(g * GROUP + r + 1) * HEAD_DIM].astype(F32)
             for r in range(GROUP)], axis=0).astype(BF16)
        s = jnp.where(jnp.concatenate([sel] * GROUP, axis=0), _dot_t(qg, kg), NEG)
        _flash_update(s, vg, m_ref, l_ref, acc_ref, g)

    sel = _sort_key(sc_ref[0]) >= thr
    kchunk = jnp.concatenate([k_refs[p][0] for p in range(pages)], axis=0).astype(BF16)
    vchunk = jnp.concatenate([v_refs[p][0] for p in range(pages)], axis=0).astype(BF16)
    for g in range(KV_HEADS):
        gsl = slice(g * HEAD_DIM, (g + 1) * HEAD_DIM)
        update(g, sel, kchunk[:, gsl], vchunk[:, gsl])

    @pl.when(c == nchunks - 1)
    def _():
        t = lax.broadcasted_iota(I32, (rows, LANES), 0)
        j = lax.broadcasted_iota(I32, (rows, LANES), 1)
        seln = jnp.where(j <= t, _sort_key(scn_ref[0]), INT_MIN) >= thr
        for g in range(KV_HEADS):
            gsl = slice(g * HEAD_DIM, (g + 1) * HEAD_DIM)
            update(g, seln, kn_ref[0][:, gsl], vn_ref[0][:, gsl])
            out = acc_ref[g] / l_ref[g]
            for r in range(GROUP):
                hsl = slice((g * GROUP + r) * HEAD_DIM, (g * GROUP + r + 1) * HEAD_DIM)
                o_ref[0, :, hsl] = out[r * rows:(r + 1) * rows, :].astype(BF16)


def _sample_attention(page_table, q, sc, thr, scn, kb_new, vb_new, cache_k, cache_v, pages):
    batch, n_pages = page_table.shape
    rows = SUBLANES
    nchunks = n_pages // pages
    seq = lambda b, c, pt: (b, 0, 0)
    page_map = lambda p: functools.partial(lambda b, c, pt, p: (pt[b, c * pages + p], 0, 0), p=p)
    k_specs = [pl.BlockSpec((1, PAGE_SIZE, ATT_KV_W), page_map(p)) for p in range(pages)]
    v_specs = [pl.BlockSpec((1, PAGE_SIZE, ATT_KV_W), page_map(p)) for p in range(pages)]
    kern = functools.partial(_sample_attn_kernel, pages=pages)
    grid_spec = pltpu.PrefetchScalarGridSpec(
        num_scalar_prefetch=1,
        grid=(batch, nchunks),
        in_specs=[pl.BlockSpec((1, rows, ATT_Q_W), seq),
                  pl.BlockSpec((1, rows, pages * PAGE_SIZE), lambda b, c, pt: (b, 0, c)),
                  pl.BlockSpec((1, rows, LANES), seq), pl.BlockSpec((1, rows, LANES), seq),
                  pl.BlockSpec((1, LANES, ATT_KV_W), seq), pl.BlockSpec((1, LANES, ATT_KV_W), seq)]
                 + k_specs + v_specs,
        out_specs=pl.BlockSpec((1, rows, ATT_Q_W), seq),
        scratch_shapes=[pltpu.VMEM((KV_HEADS, GROUP * rows, 1), F32),
                        pltpu.VMEM((KV_HEADS, GROUP * rows, 1), F32),
                        pltpu.VMEM((KV_HEADS, GROUP * rows, HEAD_DIM), F32)],
    )
    return pl.pallas_call(
        kern,
        grid_spec=grid_spec,
        out_shape=jax.ShapeDtypeStruct((batch, rows, ATT_Q_W), BF16),
        compiler_params=_cparams("arbitrary", "arbitrary"),
        name="sample_attention",
    )(page_table, q, sc, thr, scn, kb_new, vb_new, *([cache_k] * pages), *([cache_v] * pages))


def _ssm_kernel(u_ref, s0r_ref, s0i_ref, ar_ref, ai_ref, bre_ref, bim_ref, cre_ref, cim_ref, d_ref,
                wglu_ref, z_ref, sr_out, si_out, sre_ref, sim_ref, str_ref, sti_ref, *, tc, nb, lc):
    @pl.when(pl.program_id(0) == 0)
    def _():
        str_ref[...] = s0r_ref[...]
        sti_ref[...] = s0i_ref[...]

    width = u_ref.shape[-1]
    nst = sre_ref.shape[-1]
    u = u_ref[...].reshape(tc * nb, width)
    ub = u.astype(BF16)
    sre_ref[...] = jnp.dot(ub, bre_ref[...], preferred_element_type=F32)
    sim_ref[...] = jnp.dot(ub, bim_ref[...], preferred_element_type=F32)

    for ci in range(nst // lc):
        lsl = slice(ci * lc, (ci + 1) * lc)
        ar = jnp.broadcast_to(ar_ref[:, lsl], (nb, lc))
        ai = jnp.broadcast_to(ai_ref[:, lsl], (nb, lc))

        def step(j, carry):
            sr, si = carry
            r0 = pl.multiple_of(j * nb, nb)
            nr = ar * sr - ai * si + sre_ref[pl.ds(r0, nb), lsl]
            ni = ar * si + ai * sr + sim_ref[pl.ds(r0, nb), lsl]
            sre_ref[pl.ds(r0, nb), lsl] = nr
            sim_ref[pl.ds(r0, nb), lsl] = ni
            return nr, ni
        sr, si = lax.fori_loop(0, tc, step, (str_ref[:, lsl], sti_ref[:, lsl]))
        str_ref[:, lsl] = sr
        sti_ref[:, lsl] = si

    y = (jnp.dot(sre_ref[...].astype(BF16), cre_ref[...], preferred_element_type=F32)
         - jnp.dot(sim_ref[...].astype(BF16), cim_ref[...], preferred_element_type=F32)
         + u * d_ref[...])
    z = _gelu_tanh(y)
    z = z * _sigmoid(jnp.dot(z.astype(BF16), wglu_ref[...], preferred_element_type=F32))
    z_ref[...] = z.reshape(tc, nb, width).astype(BF16)
    sr_out[...] = str_ref[...]
    si_out[...] = sti_ref[...]


def _ssm_branch(u_tm, s0r, s0i, ar, ai, bre, bim, cre, cim, d, wglu, tc):
    t, nb, width = u_tm.shape
    nst = ar.shape[1]
    lc = min(256, nst)
    kern = functools.partial(_ssm_kernel, tc=tc, nb=nb, lc=lc)
    return pl.pallas_call(
        kern,
        grid=(t // tc,),
        in_specs=[pl.BlockSpec((tc, nb, width), lambda c: (c, 0, 0)),
                  _const_spec((nb, nst)), _const_spec((nb, nst)),
                  _const_spec((1, nst)), _const_spec((1, nst)),
                  _const_spec((width, nst)), _const_spec((width, nst)),
                  _const_spec((nst, width)), _const_spec((nst, width)),
                  _const_spec((1, width)), _const_spec((width, width))],
        out_specs=[pl.BlockSpec((tc, nb, width), lambda c: (c, 0, 0)),
                   pl.BlockSpec((nb, nst), lambda c: (0, 0)), pl.BlockSpec((nb, nst), lambda c: (0, 0))],
        out_shape=[jax.ShapeDtypeStruct((t, nb, width), BF16),
                   jax.ShapeDtypeStruct((nb, nst), F32), jax.ShapeDtypeStruct((nb, nst), F32)],
        scratch_shapes=[pltpu.VMEM((tc * nb, nst), F32), pltpu.VMEM((tc * nb, nst), F32),
                        pltpu.VMEM((nb, nst), F32), pltpu.VMEM((nb, nst), F32)],
        compiler_params=_cparams("arbitrary"),
        name="ssm_branch",
    )(u_tm, s0r, s0i, ar, ai, bre, bim, cre, cim, d, wglu)


MERGE_CW = 512


def _merge_kernel(o_ref, z_ref, x_ref, g_ref, wg_ref, wa_ref, ws_ref, wo_ref, out_ref, mg_ref):
    x = x_ref[...]
    d = x.shape[1]
    h = _rms(x, g_ref[...]).astype(BF16)
    o = o_ref[...]
    z = z_ref[...]
    for c in range(d // MERGE_CW):
        csl = slice(c * MERGE_CW, (c + 1) * MERGE_CW)
        ga = jnp.dot(h, wg_ref[:, csl], preferred_element_type=F32)
        gs = jnp.dot(h, wg_ref[:, d + c * MERGE_CW:d + (c + 1) * MERGE_CW], preferred_element_type=F32)
        a = jnp.dot(o, wa_ref[:, csl], preferred_element_type=F32)
        s = jnp.dot(z, ws_ref[:, csl], preferred_element_type=F32)
        mg_ref[:, csl] = (_sigmoid(ga) * a + _sigmoid(gs) * s).astype(BF16)
    out_ref[...] = x + jnp.dot(mg_ref[...], wo_ref[...], preferred_element_type=F32)


def _merge(o_att, z, z_map, x, g, wg, wa, ws, wo, tm):
    n, d = x.shape
    zw = ws.shape[0]
    row = lambda i: (i, 0)
    return pl.pallas_call(
        _merge_kernel,
        grid=(n // tm,),
        in_specs=[pl.BlockSpec((tm, ATT_Q_W), row), pl.BlockSpec((tm, zw), z_map),
                  pl.BlockSpec((tm, d), row), _const_spec((1, d)), _const_spec(wg.shape),
                  _const_spec(wa.shape), _const_spec(ws.shape), _const_spec(wo.shape)],
        out_specs=pl.BlockSpec((tm, d), row),
        out_shape=jax.ShapeDtypeStruct((n, d), F32),
        scratch_shapes=[pltpu.VMEM((tm, d), BF16)],
        compiler_params=_cparams("arbitrary"),
        name="gated_merge",
    )(o_att, z, x, g, wg, wa, ws, wo)


def _ffn_kernel(x_ref, g_ref, init_ref, wup_ref, cw_ref, cb_ref, wdn_ref, out_ref, st_ref,
                h_ref, acc_ref, ext_ref, carry_ref, *, tm, tf, shift, tiles_per_seq, off):
    i = pl.program_id(0)
    f = pl.program_id(1)

    @pl.when(f == 0)
    def _():
        h_ref[...] = _rms(x_ref[...], g_ref[...]).astype(BF16)
        acc_ref[...] = jnp.zeros(acc_ref.shape, F32)

    first = (i % tiles_per_seq) == 0

    @pl.when(first)
    def _():
        ext_ref[off - 2 * shift:off, :] = init_ref[...]

    @pl.when(jnp.logical_not(first))
    def _():
        ext_ref[off - 2 * shift:off, :] = carry_ref[f]

    def conv_piece(csl):
        up = jnp.dot(h_ref[...], wup_ref[0, :, csl], preferred_element_type=F32)
        ext_ref[off:off + tm, csl] = up
        return (cw_ref[0, 2:3, csl] * up + cw_ref[0, 1:2, csl] * ext_ref[off - shift:off - shift + tm, csl]
                + cw_ref[0, 0:1, csl] * ext_ref[off - 2 * shift:off - 2 * shift + tm, csl] + cb_ref[0, :, csl])

    down = None
    for j in range(tf // FFN_SUB):
        gate = conv_piece(slice(j * FFN_SUB, (j + 1) * FFN_SUB))
        val = conv_piece(slice(tf + j * FFN_SUB, tf + (j + 1) * FFN_SUB))
        act = (_gelu_tanh(gate) * val).astype(BF16)
        part = jnp.dot(act, wdn_ref[0, j * FFN_SUB:(j + 1) * FFN_SUB, :], preferred_element_type=F32)
        down = part if down is None else down + part
    acc_ref[...] += down
    state = ext_ref[off + tm - 2 * shift:off + tm, :]
    carry_ref[f] = state
    st_ref[0] = state

    @pl.when(f == pl.num_programs(1) - 1)
    def _():
        out_ref[...] = x_ref[...] + acc_ref[...]


def _conv_ffn(x, g, init, wup, cw, cb, wdn, tm, shift, tiles_per_seq):
    n, d = x.shape
    nf, _, tf2 = wup.shape
    tf = tf2 // 2
    hist = 2 * shift
    off = -(-hist // SUBLANES) * SUBLANES
    kern = functools.partial(_ffn_kernel, tm=tm, tf=tf, shift=shift, tiles_per_seq=tiles_per_seq, off=off)
    return pl.pallas_call(
        kern,
        grid=(n // tm, nf),
        in_specs=[pl.BlockSpec((tm, d), lambda i, f: (i, 0)), _const_spec((1, d)),
                  pl.BlockSpec((hist, tf2), lambda i, f: (0, f)),
                  pl.BlockSpec((1, d, tf2), lambda i, f: (f, 0, 0)),
                  pl.BlockSpec((1, CONV_W, tf2), lambda i, f: (f, 0, 0)),
                  pl.BlockSpec((1, 1, tf2), lambda i, f: (f, 0, 0)),
                  pl.BlockSpec((1, tf, d), lambda i, f: (f, 0, 0))],
        out_specs=[pl.BlockSpec((tm, d), lambda i, f: (i, 0)),
                   pl.BlockSpec((1, hist, tf2), lambda i, f: (i, 0, f))],
        out_shape=[jax.ShapeDtypeStruct((n, d), F32),
                   jax.ShapeDtypeStruct((n // tm, hist, nf * tf2), F32)],
        scratch_shapes=[pltpu.VMEM((tm, d), BF16), pltpu.VMEM((tm, d), F32),
                        pltpu.VMEM((off + tm, tf2), F32), pltpu.VMEM((nf, hist, tf2), F32)],
        compiler_params=_cparams("arbitrary", "arbitrary"),
        name="conv_ffn",
    )(x, g, init, wup, cw, cb, wdn)


def _ple_kernel(x_ref, p_ref, gpg_ref, wpg_ref, wple_ref, gple_ref, out_ref):
    x = x_ref[...]
    ple = _rms(jnp.dot(p_ref[...].astype(BF16), wple_ref[...], preferred_element_type=F32), gple_ref[...])
    gate = _sigmoid(jnp.dot(_rms(x, gpg_ref[...]).astype(BF16), wpg_ref[...], preferred_element_type=F32))
    out_ref[...] = x + gate * ple


def _ple(x, p, gpg, wpg, wple, gple, tm):
    n, d = x.shape
    pd = p.shape[1]
    row = lambda i: (i, 0)
    return pl.pallas_call(
        _ple_kernel,
        grid=(n // tm,),
        in_specs=[pl.BlockSpec((tm, d), row), pl.BlockSpec((tm, pd), row), _const_spec((1, d)),
                  _const_spec((d, d)), _const_spec((pd, d)), _const_spec((1, d))],
        out_specs=pl.BlockSpec((tm, d), row),
        out_shape=jax.ShapeDtypeStruct((n, d), F32),
        compiler_params=_cparams("arbitrary"),
        name="ple_gate",
    )(x, p, gpg, wpg, wple, gple)


def _rope_tables(pos, head_dim, rot_dim):
    half = rot_dim // 2
    inv_freq = jnp.asarray(
        np.float32(ROPE_THETA) ** (-np.arange(half, dtype=np.float32) / np.float32(half)))
    ang = pos.astype(F32)[:, None] * inv_freq[None, :]
    cos, sin = jnp.cos(ang), jnp.sin(ang)
    n = pos.shape[0]
    ones = jnp.ones((n, head_dim - rot_dim), F32)
    zeros = jnp.zeros((n, head_dim - rot_dim), F32)
    zh = jnp.zeros((n, half), F32)
    c = jnp.concatenate([cos, cos, ones], axis=1)
    sa = jnp.concatenate([zh, sin, zeros], axis=1)
    sb = jnp.concatenate([-sin, zh, zeros], axis=1)
    reps = LANES // head_dim
    return jnp.stack([jnp.tile(t, (1, reps)) for t in (c, sa, sb)], axis=0)


def _ssm_params(lw):
    a_re = lw['ssm_a_re'].astype(F32)
    a_im = lw['ssm_a_im'].astype(F32)
    groups, nstate = a_re.shape
    dt = jnp.exp(lw['ssm_log_dt'].astype(F32))[:, None]
    mag = jnp.exp(a_re * dt)
    ang = a_im * dt
    ab_re, ab_im = mag * jnp.cos(ang), mag * jnp.sin(ang)
    den = a_re * a_re + a_im * a_im
    f_re = ((ab_re - 1.0) * a_re + ab_im * a_im) / den
    f_im = (ab_im * a_re - (ab_re - 1.0) * a_im) / den
    b_re = lw['ssm_b_re'].astype(F32)
    b_im = lw['ssm_b_im'].astype(F32)
    bb_re = f_re[..., None] * b_re - f_im[..., None] * b_im
    bb_im = f_re[..., None] * b_im + f_im[..., None] * b_re
    eye = jnp.eye(groups, dtype=F32)

    def in_map(bb):
        m = jnp.einsum('gnc,gh->gchn', bb, eye)
        return m.reshape(groups * SSM_GROUP, groups * nstate).astype(BF16)

    def out_map(cc):
        m = jnp.einsum('gcn,gh->gnhc', cc.astype(F32), eye)
        return m.reshape(groups * nstate, groups * SSM_GROUP).astype(BF16)

    return dict(ar=ab_re.reshape(1, -1), ai=ab_im.reshape(1, -1),
                bre=in_map(bb_re), bim=in_map(bb_im),
                cre=out_map(lw['ssm_c_re']), cim=out_map(lw['ssm_c_im']),
                d=lw['ssm_d'].astype(F32).reshape(1, -1), wglu=lw['w_glu'].astype(BF16))


def _chunk_features(a, d_ff, tf):
    nf = -(-d_ff // tf)
    pad = [(0, 0)] * (a.ndim - 1) + [(0, nf * tf - d_ff)]
    gate = jnp.pad(a[..., :d_ff], pad).reshape(a.shape[:-1] + (nf, tf))
    val = jnp.pad(a[..., d_ff:], pad).reshape(a.shape[:-1] + (nf, tf))
    return jnp.concatenate([gate, val], axis=-1)


def _unchunk_features(a, d_ff, tf):
    nf = a.shape[-1] // (2 * tf)
    a = a.reshape(a.shape[:-1] + (nf, 2, tf))
    gate = a[..., 0, :].reshape(a.shape[:-3] + (nf * tf,))[..., :d_ff]
    val = a[..., 1, :].reshape(a.shape[:-3] + (nf * tf,))[..., :d_ff]
    return jnp.concatenate([gate, val], axis=-1)


FFN_TF = 512
FFN_SUB = 256


def _layer_weights(lw):
    w_in = lw['w_in']
    d = w_in.shape[0]
    o = np.cumsum([0, ATT_Q_W, ATT_KV_W, ATT_KV_W, IDX_Q_W, IDX_DIM, IDX_HEADS])
    ssm_w = lw['ssm_d'].shape[0]
    w_q_kv = w_in[:, :o[3]]
    w_iq, w_ik, w_iw = w_in[:, o[3]:o[4]], w_in[:, o[4]:o[5]], w_in[:, o[5]:o[6]]
    w_u = w_in[:, o[6]:o[6] + ssm_w]
    w_gates = w_in[:, o[6] + ssm_w:]
    w_idx = jnp.concatenate([w_iq, w_ik, w_ik, w_iw, jnp.zeros((d, LANES - IDX_HEADS), w_in.dtype)], axis=1)
    d_ff = lw['w_down'].shape[0]
    nf = -(-d_ff // FFN_TF)
    wup = jnp.transpose(_chunk_features(lw['w_up'], d_ff, FFN_TF), (1, 0, 2)).astype(BF16)
    wdn = jnp.pad(lw['w_down'], ((0, nf * FFN_TF - d_ff), (0, 0))).reshape(nf, FFN_TF, d).astype(BF16)
    cw = jnp.transpose(_chunk_features(lw['conv_w'], d_ff, FFN_TF), (1, 0, 2)).astype(F32)
    cb = _chunk_features(lw['conv_b'], d_ff, FFN_TF).reshape(nf, 1, 2 * FFN_TF).astype(F32)
    out = dict(
        g_mix=lw['g_mix'].reshape(1, d), g_q=lw['g_q'].reshape(1, HEAD_DIM), g_k=lw['g_k'].reshape(1, HEAD_DIM),
        w_qkv=w_q_kv.astype(BF16), w_idx=w_idx.astype(BF16), w_u=w_u.astype(BF16),
        w_gates=w_gates.astype(BF16),
        w_att_br=lw['w_att_br'].astype(BF16), w_ssm_br=lw['w_ssm_br'].astype(BF16), w_o=lw['w_o'].astype(BF16),
        g_ffn=lw['g_ffn'].reshape(1, d), wup=wup, wdn=wdn, cw=cw, cb=cb, d_ff=d_ff,
        g_pg=lw['g_pg'].reshape(1, d), w_pg=lw['w_pg'].astype(BF16), w_ple=lw['w_ple'].astype(BF16),
        g_ple=lw['g_ple'].reshape(1, d))
    out.update(_ssm_params(lw))
    return out


def _row_tile(n, cap):
    return cap if n % cap == 0 else n


def _prompt_layer(x3, p3, w):
    batch, seq, d = x3.shape
    n = batch * seq
    x = x3.reshape(n, d)
    tm = _row_tile(seq, 512)
    tpb = seq // tm
    pos = jnp.arange(seq)
    tab_att = _rope_tables(pos, HEAD_DIM, ROT_DIM)
    tab_idx = _rope_tables(pos, IDX_DIM, IDX_ROT_DIM)

    q, k, v, kb, vb = _qkv_proj(x, w['g_mix'], w['w_qkv'], w['g_q'], w['g_k'], tab_att, tm)
    iq, ik2, ikb, iw = _idx_proj(x, w['g_mix'], w['w_idx'], tab_idx, tm)
    ssm_w = w['w_u'].shape[1]
    u_tm = _norm_matmul(x, w['g_mix'], w['w_u'], tm, ssm_w,
                        jax.ShapeDtypeStruct((seq, batch * ssm_w), F32),
                        lambda i, j: (i % tpb, i // tpb), "u_proj").reshape(seq, batch, ssm_w)

    o_att = _prompt_attention(q, iq, iw, ikb, kb, vb, batch, seq)

    nst = w['ar'].shape[1]
    zeros = jnp.zeros((batch, nst), F32)
    z_tm, s_re, s_im = _ssm_branch(u_tm, zeros, zeros, w['ar'], w['ai'], w['bre'], w['bim'],
                                   w['cre'], w['cim'], w['d'], w['wglu'], tc=min(32, seq))
    tmm = _row_tile(seq, 256)
    tpbm = seq // tmm
    x1 = _merge(o_att, z_tm.reshape(seq, batch * ssm_w), lambda i: (i % tpbm, i // tpbm),
                x, w['g_mix'], w['w_gates'], w['w_att_br'], w['w_ssm_br'], w['w_o'], tmm)

    nf, _, tf2 = w['wup'].shape
    x2, conv = _conv_ffn(x1, w['g_ffn'], jnp.zeros((CONV_W - 1, nf * tf2), F32),
                         w['wup'], w['cw'], w['cb'], w['wdn'], tm, 1, tpb)
    conv = _unchunk_features(conv[tpb - 1::tpb], w['d_ff'], tf2 // 2)
    y = _ple(x2, p3.reshape(n, -1), w['g_pg'], w['w_pg'], w['w_ple'], w['g_ple'], tm)

    groups = nst // SSM_STATE
    return (y.reshape(batch, seq, d), k.reshape(batch, seq, KV_HEADS, HEAD_DIM),
            v.reshape(batch, seq, KV_HEADS, HEAD_DIM), ik2[:, :IDX_DIM].reshape(batch, seq, IDX_DIM),
            s_re.reshape(batch, groups, SSM_STATE), s_im.reshape(batch, groups, SSM_STATE), conv)


def _pad_rows(a, rows):
    pad = [(0, 0)] * a.ndim
    pad[1] = (0, rows - a.shape[1])
    return jnp.pad(a, pad)


def _sample_layer(x3, p3, cache_k, cache_v, cache_ik, s0_re, s0_im, conv0, page_table, w):
    batch, t_new, d = x3.shape
    n = batch * t_new
    n_pages = page_table.shape[1]
    past = n_pages * PAGE_SIZE
    x = x3.reshape(n, d)
    pos = past + (jnp.arange(n) % t_new)
    tab_att = _rope_tables(pos, HEAD_DIM, ROT_DIM)
    tab_idx = _rope_tables(pos, IDX_DIM, IDX_ROT_DIM)

    q, k, v, kb, vb = _qkv_proj(x, w['g_mix'], w['w_qkv'], w['g_q'], w['g_k'], tab_att, n)
    iq, ik2, ikb, iw = _idx_proj(x, w['g_mix'], w['w_idx'], tab_idx, n)
    ssm_w = w['w_u'].shape[1]
    u = _norm_matmul(x, w['g_mix'], w['w_u'], n, ssm_w,
                     jax.ShapeDtypeStruct((n, ssm_w), F32), lambda i, j: (i, j), "u_proj_s")

    per_seq = lambda a: a.reshape(batch, t_new, a.shape[-1])
    pages = 8 if n_pages % 8 == 0 else 1
    sc, thr, scn = _sample_scores(
        page_table, _pad_rows(per_seq(iq), SUBLANES), _pad_rows(per_seq(iw), SUBLANES),
        _pad_rows(per_seq(ikb), LANES), cache_ik, pages, t_new)
    o_att = _sample_attention(
        page_table, _pad_rows(per_seq(q), SUBLANES), sc, thr, scn,
        _pad_rows(per_seq(kb), LANES), _pad_rows(per_seq(vb), LANES), cache_k, cache_v, pages)
    o_att = o_att[:, :t_new].reshape(n, ATT_Q_W)

    nst = w['ar'].shape[1]
    u_tm = jnp.transpose(u.reshape(batch, t_new, ssm_w), (1, 0, 2))
    z_tm, s_re, s_im = _ssm_branch(u_tm, s0_re.reshape(batch, nst).astype(F32),
                                   s0_im.reshape(batch, nst).astype(F32), w['ar'], w['ai'], w['bre'],
                                   w['bim'], w['cre'], w['cim'], w['d'], w['wglu'], tc=t_new)
    z = jnp.transpose(z_tm, (1, 0, 2)).reshape(n, ssm_w)
    x1 = _merge(o_att, z, lambda i: (i, 0), x, w['g_mix'], w['w_gates'], w['w_att_br'], w['w_ssm_br'],
                w['w_o'], n)

    nf, _, tf2 = w['wup'].shape
    d_ff = w['d_ff']
    x1_tm = jnp.transpose(x1.reshape(batch, t_new, d), (1, 0, 2)).reshape(n, d)
    init = _chunk_features(jnp.transpose(conv0.astype(F32), (1, 0, 2)), d_ff, tf2 // 2)
    init = init.reshape((CONV_W - 1) * batch, nf * tf2)
    x2_tm, conv = _conv_ffn(x1_tm, w['g_ffn'], init, w['wup'], w['cw'], w['cb'], w['wdn'], n, batch, 1)
    conv = _unchunk_features(conv.reshape(CONV_W - 1, batch, nf * tf2), d_ff, tf2 // 2)
    conv = jnp.transpose(conv, (1, 0, 2))
    x2 = jnp.transpose(x2_tm.reshape(t_new, batch, d), (1, 0, 2)).reshape(n, d)
    y = _ple(x2, p3.reshape(n, -1), w['g_pg'], w['w_pg'], w['w_ple'], w['g_ple'], n)

    groups = nst // SSM_STATE
    return (y.reshape(batch, t_new, d), k.reshape(batch, t_new, KV_HEADS, HEAD_DIM),
            v.reshape(batch, t_new, KV_HEADS, HEAD_DIM), ik2[:, :IDX_DIM].reshape(batch, t_new, IDX_DIM),
            s_re.reshape(batch, groups, SSM_STATE), s_im.reshape(batch, groups, SSM_STATE), conv)


def kernel(x_prompt, x_sample, cache_k, cache_v, cache_idx_k, state_ssm_re, state_ssm_im, state_conv, page_table, p_prompt, p_sample, g_mix, w_in, g_q, g_k, ssm_a_re, ssm_a_im, ssm_log_dt, ssm_b_re, ssm_b_im, ssm_c_re, ssm_c_im, ssm_d, w_glu, w_att_br, w_ssm_br, w_o, g_ffn, w_up, conv_w, conv_b, w_down, w_ple, g_ple, g_pg, w_pg):
    depth = w_in.shape[0]
    n_pool = cache_k.shape[1]
    pool_k = cache_k.reshape(depth * n_pool, PAGE_SIZE, ATT_KV_W)
    pool_v = cache_v.reshape(depth * n_pool, PAGE_SIZE, ATT_KV_W)
    pool_ik = cache_idx_k.reshape(depth * n_pool, PAGE_SIZE, IDX_DIM)
    xp, xs = x_prompt, x_sample
    outs_p, outs_s = [], []
    for i in range(depth):
        lw = dict(g_mix=g_mix[i], w_in=w_in[i], g_q=g_q[i], g_k=g_k[i],
                  ssm_a_re=ssm_a_re[i], ssm_a_im=ssm_a_im[i], ssm_log_dt=ssm_log_dt[i],
                  ssm_b_re=ssm_b_re[i], ssm_b_im=ssm_b_im[i], ssm_c_re=ssm_c_re[i], ssm_c_im=ssm_c_im[i],
                  ssm_d=ssm_d[i], w_glu=w_glu[i], w_att_br=w_att_br[i], w_ssm_br=w_ssm_br[i], w_o=w_o[i],
                  g_ffn=g_ffn[i], w_up=w_up[i], conv_w=conv_w[i], conv_b=conv_b[i], w_down=w_down[i],
                  w_ple=w_ple[i], g_ple=g_ple[i], g_pg=g_pg[i], w_pg=w_pg[i])
        w = _layer_weights(lw)
        xp, *rest_p = _prompt_layer(xp, p_prompt[i], w)
        outs_p.append(rest_p)
        xs, *rest_s = _sample_layer(xs, p_sample[i], pool_k, pool_v, pool_ik, state_ssm_re[i],
                                    state_ssm_im[i], state_conv[i], page_table + i * n_pool, w)
        outs_s.append(rest_s)
    stack = lambda outs, j: jnp.stack([o[j] for o in outs], 0)
    return ((xp, xs) + tuple(stack(outs_p, j) for j in range(6))
            + tuple(stack(outs_s, j) for j in range(6)))
```

```python
import functools

import numpy as np
import jax
import jax.numpy as jnp
from jax import lax
from jax.experimental import pallas as pl
from jax.experimental.pallas import tpu as pltpu

F32 = jnp.float32
BF16 = jnp.bfloat16
I32 = jnp.int32

ATT_HEADS = 8
KV_HEADS = 2
HEAD_DIM = 128
GROUP = ATT_HEADS // KV_HEADS
ROT_DIM = HEAD_DIM // 4
ROPE_THETA = 500000.0
IDX_HEADS = 16
IDX_DIM = 64
IDX_ROT_DIM = IDX_DIM // 4
IDX_SCALE = (IDX_HEADS ** -0.5) * (IDX_DIM ** -0.5)
TOPK_MAX = 256
Q_BLOCK = 128
PAGE_SIZE = 128
SSM_GROUP = 16
SSM_STATE = 64
CONV_W = 3
EPS = 1e-6

LANES = 128
SUBLANES = 8
NEG = -1e30
INT_MIN = -2 ** 31
VMEM_LIMIT = 56 * 1024 * 1024

ATT_Q_W = ATT_HEADS * HEAD_DIM
ATT_KV_W = KV_HEADS * HEAD_DIM
IDX_Q_W = IDX_HEADS * IDX_DIM


def _cparams(*sem):
    return pltpu.CompilerParams(dimension_semantics=sem, vmem_limit_bytes=VMEM_LIMIT)


def _const_spec(shape):
    nd = len(shape)
    return pl.BlockSpec(shape, lambda *_: (0,) * nd, pipeline_mode=pl.Buffered(1))


def _rms(x, g):
    ms = jnp.mean(x * x, axis=-1, keepdims=True)
    return (x * lax.rsqrt(ms + EPS)) * g


def _gelu_tanh(x):
    return 0.5 * x * (1.0 + jnp.tanh(0.7978845608028654 * (x + 0.044715 * (x * x * x))))


def _sigmoid(x):
    return 1.0 / (1.0 + jnp.exp(-x))


def _rope(x, tab_ref, half):
    return (x * tab_ref[0] + pltpu.roll(x, half, 1) * tab_ref[1]
            + pltpu.roll(x, LANES - half, 1) * tab_ref[2])


def _dot_t(a, b):
    return lax.dot_general(a, b, (((1,), (1,)), ((), ())), preferred_element_type=F32)


def _qkv_kernel(x_ref, g_ref, w_ref, gq_ref, gk_ref, tab_ref, q_ref, k_ref, v_ref, kb_ref, vb_ref):
    h = _rms(x_ref[...], g_ref[...]).astype(BF16)
    y = jnp.dot(h, w_ref[...], preferred_element_type=F32)
    half = ROT_DIM // 2
    for hh in range(ATT_HEADS):
        sl = slice(hh * HEAD_DIM, (hh + 1) * HEAD_DIM)
        qh = _rope(_rms(y[:, sl], gq_ref[...]), tab_ref, half)
        q_ref[:, sl] = (qh * (HEAD_DIM ** -0.5)).astype(BF16)
    for hh in range(KV_HEADS):
        sl = slice(hh * HEAD_DIM, (hh + 1) * HEAD_DIM)
        kh = _rope(_rms(y[:, ATT_Q_W + hh * HEAD_DIM:ATT_Q_W + (hh + 1) * HEAD_DIM], gk_ref[...]),
                   tab_ref, half)
        k_ref[:, sl] = kh
        kb_ref[:, sl] = kh.astype(BF16)
    vv = y[:, ATT_Q_W + ATT_KV_W:]
    v_ref[...] = vv
    vb_ref[...] = vv.astype(BF16)


def _qkv_proj(x, g, w, gq, gk, tab, tm):
    n, d = x.shape
    wd = w.shape[1]
    p_blocks = tab.shape[1] // tm
    row = lambda i: (i, 0)
    return pl.pallas_call(
        _qkv_kernel,
        grid=(n // tm,),
        in_specs=[pl.BlockSpec((tm, d), row), _const_spec((1, d)), _const_spec((d, wd)),
                  _const_spec((1, HEAD_DIM)), _const_spec((1, HEAD_DIM)),
                  pl.BlockSpec((3, tm, LANES), lambda i: (0, i % p_blocks, 0))],
        out_specs=[pl.BlockSpec((tm, ATT_Q_W), row), pl.BlockSpec((tm, ATT_KV_W), row),
                   pl.BlockSpec((tm, ATT_KV_W), row), pl.BlockSpec((tm, ATT_KV_W), row),
                   pl.BlockSpec((tm, ATT_KV_W), row)],
        out_shape=[jax.ShapeDtypeStruct((n, ATT_Q_W), BF16), jax.ShapeDtypeStruct((n, ATT_KV_W), F32),
                   jax.ShapeDtypeStruct((n, ATT_KV_W), F32), jax.ShapeDtypeStruct((n, ATT_KV_W), BF16),
                   jax.ShapeDtypeStruct((n, ATT_KV_W), BF16)],
        compiler_params=_cparams("arbitrary"),
        name="qkv_proj",
    )(x, g, w, gq, gk, tab)


def _idx_kernel(x_ref, g_ref, w_ref, tab_ref, iq_ref, ik_ref, ikb_ref, iw_ref):
    h = _rms(x_ref[...], g_ref[...]).astype(BF16)
    y = jnp.dot(h, w_ref[...], preferred_element_type=F32)
    half = IDX_ROT_DIM // 2
    for s in range(IDX_Q_W // LANES):
        sl = slice(s * LANES, (s + 1) * LANES)
        iq_ref[:, sl] = _rope(y[:, sl], tab_ref, half).astype(BF16)
    ik2 = _rope(y[:, IDX_Q_W:IDX_Q_W + LANES], tab_ref, half)
    ik_ref[...] = ik2
    ikb_ref[...] = ik2.astype(BF16)
    iw_ref[...] = y[:, IDX_Q_W + LANES:] * IDX_SCALE


def _idx_proj(x, g, w, tab, tm):
    n, d = x.shape
    wd = w.shape[1]
    p_blocks = tab.shape[1] // tm
    row = lambda i: (i, 0)
    return pl.pallas_call(
        _idx_kernel,
        grid=(n // tm,),
        in_specs=[pl.BlockSpec((tm, d), row), _const_spec((1, d)), _const_spec((d, wd)),
                  pl.BlockSpec((3, tm, LANES), lambda i: (0, i % p_blocks, 0))],
        out_specs=[pl.BlockSpec((tm, IDX_Q_W), row), pl.BlockSpec((tm, LANES), row),
                   pl.BlockSpec((tm, LANES), row), pl.BlockSpec((tm, LANES), row)],
        out_shape=[jax.ShapeDtypeStruct((n, IDX_Q_W), BF16), jax.ShapeDtypeStruct((n, LANES), F32),
                   jax.ShapeDtypeStruct((n, LANES), BF16), jax.ShapeDtypeStruct((n, LANES), F32)],
        compiler_params=_cparams("arbitrary"),
        name="idx_proj",
    )(x, g, w, tab)


def _norm_mm_kernel(x_ref, g_ref, w_ref, o_ref, h_ref):
    @pl.when(pl.program_id(1) == 0)
    def _():
        h_ref[...] = _rms(x_ref[...], g_ref[...]).astype(BF16)
    o_ref[...] = jnp.dot(h_ref[...], w_ref[...], preferred_element_type=F32).astype(o_ref.dtype)


def _norm_matmul(x, g, w, tm, tn, out_shape, out_map, name):
    n, d = x.shape
    return pl.pallas_call(
        _norm_mm_kernel,
        grid=(n // tm, w.shape[1] // tn),
        in_specs=[pl.BlockSpec((tm, d), lambda i, j: (i, 0)), _const_spec((1, d)),
                  pl.BlockSpec((d, tn), lambda i, j: (0, j))],
        out_specs=pl.BlockSpec((tm, tn), out_map),
        out_shape=out_shape,
        scratch_shapes=[pltpu.VMEM((tm, d), BF16)],
        compiler_params=_cparams("arbitrary", "arbitrary"),
        name=name,
    )(x, g, w)


def _sort_key(sc):
    bits = pltpu.bitcast(sc, I32)
    return bits ^ ((bits >> 31) & jnp.int32(0x7FFFFFFF))


def _kth_largest(count_ge, shape, topk):
    def body(it, t):
        cand = t + lax.shift_left(jnp.int32(1), 31 - it)
        return jnp.where(count_ge(cand) >= topk, cand, t)
    t = lax.fori_loop(0, 32, body, jnp.full(shape, INT_MIN, I32))
    return jnp.maximum(t, INT_MIN + 1)


def _row_groups(op, x):
    parts = [x[r:r + SUBLANES, :] for r in range(0, x.shape[0], SUBLANES)]
    while len(parts) > 1:
        nxt = [op(parts[j], parts[j + 1]) for j in range(0, len(parts) - 1, 2)]
        if len(parts) % 2:
            nxt.append(parts[-1])
        parts = nxt
    return parts[0]


def _indexer_operands(iq, iw, lhs_ref, wb_ref, rows):
    for h in range(IDX_HEADS):
        lhs_ref[h * rows:(h + 1) * rows, :] = iq[:, h * IDX_DIM:(h + 1) * IDX_DIM].astype(F32)
        wb_ref[h * rows:(h + 1) * rows, :] = jnp.broadcast_to(iw[:, h:h + 1], (rows, LANES))


def _indexer_scores(lhs_ref, wb_ref, keys_t, rows):
    n_keys = keys_t.shape[1]
    r = jnp.dot(lhs_ref[...].astype(BF16), keys_t, preferred_element_type=F32)
    sc = jnp.zeros((rows, n_keys), F32)
    for h in range(IDX_HEADS):
        wh = jnp.concatenate([wb_ref[h * rows:(h + 1) * rows, :]] * (n_keys // LANES), axis=1)
        sc = sc + wh * jnp.maximum(r[h * rows:(h + 1) * rows, :], 0.0)
    return sc


def _prompt_attn_kernel(q_ref, iq_ref, iw_ref, ikb_ref, kb_ref, vb_ref, o_ref,
                        lhs_ref, q2_ref, vt_ref, key_ref, s_ref, m_ref, l_ref, acc_ref, *, seq, kc, topk):
    qb = Q_BLOCK
    npair = ATT_HEADS // 2
    i = pl.program_id(1)
    t0 = i * qb
    nch = (t0 + qb + kc - 1) // kc

    @pl.when(i == 0)
    def _():
        def transpose_chunk(c, carry):
            c0 = pl.multiple_of(c * kc, kc)
            vt_ref[c] = vb_ref[pl.ds(c0, kc), :].astype(F32).T.astype(BF16)
            return carry
        lax.fori_loop(0, seq // kc, transpose_chunk, 0)

    lane = lax.broadcasted_iota(I32, (qb, LANES), 1)
    for s in range(IDX_Q_W // LANES):
        x = iq_ref[:, s * LANES:(s + 1) * LANES].astype(F32)
        lhs_ref[(2 * s) * qb:(2 * s + 1) * qb, :] = jnp.where(lane < IDX_DIM, x, 0.0).astype(BF16)
        lhs_ref[(2 * s + 1) * qb:(2 * s + 2) * qb, :] = jnp.where(lane >= IDX_DIM, x, 0.0).astype(BF16)
    iwt = iw_ref[...].T
    for p in range(npair):
        for r in range(2):
            hsl = slice((2 * p + r) * HEAD_DIM, (2 * p + r + 1) * HEAD_DIM)
            q2_ref[p, r * qb:(r + 1) * qb, :] = q_ref[:, hsl]

    key_pos = lax.broadcasted_iota(I32, (kc, LANES), 0)
    q_pos = t0 + lax.broadcasted_iota(I32, (kc, LANES), 1)

    def score_chunk(c, carry):
        c0 = pl.multiple_of(c * kc, kc)
        ikc = ikb_ref[pl.ds(c0, kc), :]
        sc = jnp.zeros((kc, LANES), F32)
        for s in range(IDX_HEADS // 2):
            r = _dot_t(ikc, lhs_ref[2 * s * qb:(2 * s + 2) * qb, :])
            sc = (sc + iwt[2 * s:2 * s + 1, :] * jnp.maximum(r[:, :LANES], 0.0)
                  + iwt[2 * s + 1:2 * s + 2, :] * jnp.maximum(r[:, LANES:], 0.0))
        key_ref[c] = jnp.where(c0 + key_pos <= q_pos, _sort_key(sc), INT_MIN)
        return carry
    lax.fori_loop(0, nch, score_chunk, 0)

    def count_ge(cand):
        def chunk(c, acc):
            return acc + _row_groups(jnp.add, jnp.where(key_ref[c] >= cand, 1.0, 0.0))
        acc = lax.fori_loop(0, nch, chunk, jnp.zeros((SUBLANES, LANES), F32))
        return jnp.sum(acc, axis=0, keepdims=True)
    thr = _kth_largest(count_ge, (1, LANES), float(topk))

    m_ref[...] = jnp.full(m_ref.shape, NEG, F32)
    l_ref[...] = jnp.zeros(l_ref.shape, F32)
    acc_ref[...] = jnp.zeros(acc_ref.shape, F32)

    def score_pass(c, carry):
        c0 = pl.multiple_of(c * kc, kc)
        sel = key_ref[c] >= thr
        for p in range(npair):
            gsl = slice((2 * p) // GROUP * HEAD_DIM, ((2 * p) // GROUP + 1) * HEAD_DIM)
            s = _dot_t(kb_ref[pl.ds(c0, kc), gsl], q2_ref[p])
            s = jnp.concatenate([jnp.where(sel, s[:, :LANES], NEG), jnp.where(sel, s[:, LANES:], NEG)], axis=1)
            s_ref[c, p] = s
            m_ref[p] = jnp.maximum(m_ref[p], _row_groups(jnp.maximum, s))
        return carry
    lax.fori_loop(0, nch, score_pass, 0)
    m = [jnp.max(m_ref[p], axis=0, keepdims=True) for p in range(npair)]

    def value_pass(c, carry):
        for p in range(npair):
            gsl = slice((2 * p) // GROUP * HEAD_DIM, ((2 * p) // GROUP + 1) * HEAD_DIM)
            pr = jnp.exp(s_ref[c, p] - m[p])
            l_ref[p] += _row_groups(jnp.add, pr)
            acc_ref[p] += jnp.dot(vt_ref[c, gsl, :], pr.astype(BF16),
                                  preferred_element_type=F32)
        return carry
    lax.fori_loop(0, nch, value_pass, 0)

    for p in range(npair):
        o_t = acc_ref[p] / jnp.sum(l_ref[p], axis=0, keepdims=True)
        for r in range(2):
            hsl = slice((2 * p + r) * HEAD_DIM, (2 * p + r + 1) * HEAD_DIM)
            o_ref[:, hsl] = o_t[:, r * qb:(r + 1) * qb].T.astype(BF16)


def _prompt_attention(q, iq, iw, ikb, kb, vb, batch, seq):
    kc = min(256, seq)
    topk = min(TOPK_MAX, seq // 4)
    nq = seq // Q_BLOCK
    blk = lambda b, i: (b * nq + i, 0)
    full = lambda b, i: (b, 0)
    kern = functools.partial(_prompt_attn_kernel, seq=seq, kc=kc, topk=topk)
    return pl.pallas_call(
        kern,
        grid=(batch, nq),
        in_specs=[pl.BlockSpec((Q_BLOCK, ATT_Q_W), blk), pl.BlockSpec((Q_BLOCK, IDX_Q_W), blk),
                  pl.BlockSpec((Q_BLOCK, LANES), blk), pl.BlockSpec((seq, LANES), full),
                  pl.BlockSpec((seq, ATT_KV_W), full), pl.BlockSpec((seq, ATT_KV_W), full)],
        out_specs=pl.BlockSpec((Q_BLOCK, ATT_Q_W), blk),
        out_shape=jax.ShapeDtypeStruct((batch * seq, ATT_Q_W), BF16),
        scratch_shapes=[pltpu.VMEM((IDX_HEADS * Q_BLOCK, LANES), BF16),
                        pltpu.VMEM((ATT_HEADS // 2, 2 * Q_BLOCK, HEAD_DIM), BF16),
                        pltpu.VMEM((seq // kc, ATT_KV_W, kc), BF16),
                        pltpu.VMEM((seq // kc, kc, Q_BLOCK), I32),
                        pltpu.VMEM((seq // kc, ATT_HEADS // 2, kc, 2 * Q_BLOCK), F32),
                        pltpu.VMEM((ATT_HEADS // 2, SUBLANES, 2 * Q_BLOCK), F32),
                        pltpu.VMEM((ATT_HEADS // 2, SUBLANES, 2 * Q_BLOCK), F32),
                        pltpu.VMEM((ATT_HEADS // 2, HEAD_DIM, 2 * Q_BLOCK), F32)],
        compiler_params=_cparams("arbitrary", "arbitrary"),
        name="prompt_attention",
    )(q, iq, iw, ikb, kb, vb)


def _sample_score_kernel(pt_ref, iq_ref, iw_ref, ikn_ref, *rest, pages, n_new):
    page_refs = rest[:pages]
    sc_ref, thr_ref, scn_ref, lhs_ref, wb_ref, key_ref = rest[pages:]
    del pt_ref
    c = pl.program_id(1)
    nchunks = pl.num_programs(1)
    rows = SUBLANES

    @pl.when(c == 0)
    def _():
        _indexer_operands(iq_ref[0], iw_ref[0], lhs_ref, wb_ref, rows)

    keys_t = jnp.concatenate([page_refs[p][0] for p in range(pages)], axis=1).astype(BF16)
    sc = _indexer_scores(lhs_ref, wb_ref, keys_t, rows)
    sc_ref[0] = sc
    key = _sort_key(sc)
    for p in range(pages):
        key_ref[c * pages + p] = key[:, p * PAGE_SIZE:(p + 1) * PAGE_SIZE]

    @pl.when(c == nchunks - 1)
    def _():
        scn = _indexer_scores(lhs_ref, wb_ref, ikn_ref[0], rows)
        scn_ref[0] = scn
        t = lax.broadcasted_iota(I32, (rows, LANES), 0)
        j = lax.broadcasted_iota(I32, (rows, LANES), 1)
        keyn = jnp.where(j <= t, _sort_key(scn), INT_MIN)
        total = key_ref.shape[0]

        def count_ge(cand):
            hit = jnp.sum(jnp.where(key_ref[...] >= cand[None], 1.0, 0.0), axis=0)
            return jnp.sum(hit + jnp.where(keyn >= cand, 1.0, 0.0), axis=1, keepdims=True)
        topk = float(min(TOPK_MAX, (total * PAGE_SIZE + n_new) // 4))
        thr = _kth_largest(count_ge, (rows, 1), topk)
        thr_ref[0] = jnp.broadcast_to(thr, (rows, LANES))


def _sample_scores(page_table, iq, iw, ikb_new, cache_ik, pages, n_new):
    batch, n_pages = page_table.shape
    rows = SUBLANES
    past = n_pages * PAGE_SIZE
    nchunks = n_pages // pages
    seq = lambda b, c, pt: (b, 0, 0)
    page_specs = [pl.BlockSpec((1, IDX_DIM, PAGE_SIZE),
                               functools.partial(lambda b, c, pt, p: (pt[b, c * pages + p], 0, 0), p=p))
                  for p in range(pages)]
    kern = functools.partial(_sample_score_kernel, pages=pages, n_new=n_new)
    grid_spec = pltpu.PrefetchScalarGridSpec(
        num_scalar_prefetch=1,
        grid=(batch, nchunks),
        in_specs=[pl.BlockSpec((1, rows, IDX_Q_W), seq), pl.BlockSpec((1, rows, LANES), seq),
                  pl.BlockSpec((1, IDX_DIM, LANES), seq)] + page_specs,
        out_specs=[pl.BlockSpec((1, rows, pages * PAGE_SIZE), lambda b, c, pt: (b, 0, c)),
                   pl.BlockSpec((1, rows, LANES), seq), pl.BlockSpec((1, rows, LANES), seq)],
        scratch_shapes=[pltpu.VMEM((IDX_HEADS * rows, IDX_DIM), F32),
                        pltpu.VMEM((IDX_HEADS * rows, LANES), F32),
                        pltpu.VMEM((n_pages, rows, PAGE_SIZE), I32)],
    )
    return pl.pallas_call(
        kern,
        grid_spec=grid_spec,
        out_shape=[jax.ShapeDtypeStruct((batch, rows, past), F32),
                   jax.ShapeDtypeStruct((batch, rows, LANES), I32),
                   jax.ShapeDtypeStruct((batch, rows, LANES), F32)],
        compiler_params=_cparams("arbitrary", "arbitrary"),
        name="sample_scores",
    )(page_table, iq, iw, ikb_new, *([cache_ik] * pages))


def _sample_attn_kernel(pt_ref, q_ref, sc_ref, thr_ref, scn_ref, kn_ref, vn_ref, *rest, pages):
    k_refs = rest[:pages]
    v_refs = rest[pages:2 * pages]
    o_ref, qa_ref, m_ref, l_ref, acc_ref = rest[2 * pages:]
    del pt_ref
    c = pl.program_id(1)
    nchunks = pl.num_programs(1)
    rows = SUBLANES
    pcols = PAGE_SIZE * KV_HEADS
    thr = thr_ref[0][:, 0:1]

    @pl.when(c == 0)
    def _():
        for h in range(ATT_HEADS):
            qa_ref[h * rows:(h + 1) * rows, :] = q_ref[0, :, h * HEAD_DIM:(h + 1) * HEAD_DIM].astype(F32)
        m_ref[...] = jnp.full(m_ref.shape, NEG, F32)
        l_ref[...] = jnp.zeros(l_ref.shape, F32)
        acc_ref[...] = jnp.zeros(acc_ref.shape, F32)

    kk = lax.broadcasted_iota(I32, (PAGE_SIZE, KV_HEADS * pcols), 0)
    cc = lax.broadcasted_iota(I32, (PAGE_SIZE, KV_HEADS * pcols), 1)
    head_bits = KV_HEADS.bit_length() - 1
    col = cc & (pcols - 1)
    expand = jnp.where(((col >> head_bits) == kk) & ((col & (KV_HEADS - 1)) == (cc >> (pcols.bit_length() - 1))),
                       1.0, 0.0).astype(BF16)

    def update(sel, kchunk, vchunk):
        n = sel.shape[1] // PAGE_SIZE
        stack = jnp.concatenate([sel[:, p * PAGE_SIZE:(p + 1) * PAGE_SIZE] for p in range(n)], axis=0)
        e = jnp.dot(stack.astype(BF16), expand, preferred_element_type=F32)
        mask = jnp.concatenate(
            [jnp.concatenate([jnp.concatenate(
                [e[p * rows:(p + 1) * rows, g * pcols:(g + 1) * pcols] for p in range(n)], axis=1)] * GROUP, axis=0)
             for g in range(KV_HEADS)], axis=0)
        s = jnp.where(mask > 0.5, _dot_t(qa_ref[...].astype(BF16), kchunk), NEG)
        m_old = m_ref[...]
        m_new = jnp.maximum(m_old, jnp.max(s, axis=1, keepdims=True))
        alpha = jnp.exp(m_old - m_new)
        p = jnp.exp(s - m_new)
        l_ref[...] = alpha * l_ref[...] + jnp.sum(p, axis=1, keepdims=True)
        acc_ref[...] = alpha * acc_ref[...] + jnp.dot(p.astype(BF16), vchunk, preferred_element_type=F32)
        m_ref[...] = m_new

    sel = jnp.where(_sort_key(sc_ref[0]) >= thr, 1.0, 0.0)
    update(sel, jnp.concatenate([k_refs[p][0] for p in range(pages)], axis=0).astype(BF16),
           jnp.concatenate([v_refs[p][0] for p in range(pages)], axis=0).astype(BF16))

    @pl.when(c == nchunks - 1)
    def _():
        t = lax.broadcasted_iota(I32, (rows, LANES), 0)
        j = lax.broadcasted_iota(I32, (rows, LANES), 1)
        seln = jnp.where(jnp.where(j <= t, _sort_key(scn_ref[0]), INT_MIN) >= thr, 1.0, 0.0)
        update(seln, kn_ref[0].astype(BF16), vn_ref[0].astype(BF16))
        out = acc_ref[...] / l_ref[...]
        for h in range(ATT_HEADS):
            o_ref[0, :, h * HEAD_DIM:(h + 1) * HEAD_DIM] = out[h * rows:(h + 1) * rows, :].astype(BF16)


def _sample_attention(page_table, q, sc, thr, scn, kb_new, vb_new, cache_k, cache_v, pages):
    batch, n_pages = page_table.shape
    rows = SUBLANES
    nchunks = n_pages // pages
    seq = lambda b, c, pt: (b, 0, 0)
    page_map = lambda p: functools.partial(lambda b, c, pt, p: (pt[b, c * pages + p], 0, 0), p=p)
    pcols = PAGE_SIZE * KV_HEADS
    k_specs = [pl.BlockSpec((1, pcols, HEAD_DIM), page_map(p)) for p in range(pages)]
    v_specs = [pl.BlockSpec((1, pcols, HEAD_DIM), page_map(p)) for p in range(pages)]
    kern = functools.partial(_sample_attn_kernel, pages=pages)
    grid_spec = pltpu.PrefetchScalarGridSpec(
        num_scalar_prefetch=1,
        grid=(batch, nchunks),
        in_specs=[pl.BlockSpec((1, rows, ATT_Q_W), seq),
                  pl.BlockSpec((1, rows, pages * PAGE_SIZE), lambda b, c, pt: (b, 0, c)),
                  pl.BlockSpec((1, rows, LANES), seq), pl.BlockSpec((1, rows, LANES), seq),
                  pl.BlockSpec((1, pcols, HEAD_DIM), seq), pl.BlockSpec((1, pcols, HEAD_DIM), seq)]
                 + k_specs + v_specs,
        out_specs=pl.BlockSpec((1, rows, ATT_Q_W), seq),
        scratch_shapes=[pltpu.VMEM((ATT_HEADS * rows, HEAD_DIM), F32),
                        pltpu.VMEM((ATT_HEADS * rows, 1), F32),
                        pltpu.VMEM((ATT_HEADS * rows, 1), F32),
                        pltpu.VMEM((ATT_HEADS * rows, HEAD_DIM), F32)],
    )
    return pl.pallas_call(
        kern,
        grid_spec=grid_spec,
        out_shape=jax.ShapeDtypeStruct((batch, rows, ATT_Q_W), BF16),
        compiler_params=_cparams("arbitrary", "arbitrary"),
        name="sample_attention",
    )(page_table, q, sc, thr, scn, kb_new, vb_new, *([cache_k] * pages), *([cache_v] * pages))


def _ssm_kernel(u_ref, s0r_ref, s0i_ref, ar_ref, ai_ref, bre_ref, bim_ref, cre_ref, cim_ref, d_ref,
                wglu_ref, z_ref, sr_out, si_out, sre_ref, sim_ref, str_ref, sti_ref, *, tc, nb, lc):
    @pl.when(pl.program_id(0) == 0)
    def _():
        str_ref[...] = s0r_ref[...]
        sti_ref[...] = s0i_ref[...]

    width = u_ref.shape[-1]
    nst = sre_ref.shape[-1]
    u = u_ref[...].reshape(tc * nb, width)
    ub = u.astype(BF16)
    sre_ref[...] = jnp.dot(ub, bre_ref[...], preferred_element_type=F32)
    sim_ref[...] = jnp.dot(ub, bim_ref[...], preferred_element_type=F32)

    for ci in range(nst // lc):
        lsl = slice(ci * lc, (ci + 1) * lc)
        ar = jnp.broadcast_to(ar_ref[:, lsl], (nb, lc))
        ai = jnp.broadcast_to(ai_ref[:, lsl], (nb, lc))

        def step(j, carry):
            sr, si = carry
            r0 = pl.multiple_of(j * nb, nb)
            nr = ar * sr - ai * si + sre_ref[pl.ds(r0, nb), lsl]
            ni = ar * si + ai * sr + sim_ref[pl.ds(r0, nb), lsl]
            sre_ref[pl.ds(r0, nb), lsl] = nr
            sim_ref[pl.ds(r0, nb), lsl] = ni
            return nr, ni
        sr, si = lax.fori_loop(0, tc, step, (str_ref[:, lsl], sti_ref[:, lsl]))
        str_ref[:, lsl] = sr
        sti_ref[:, lsl] = si

    y = (jnp.dot(sre_ref[...].astype(BF16), cre_ref[...], preferred_element_type=F32)
         - jnp.dot(sim_ref[...].astype(BF16), cim_ref[...], preferred_element_type=F32)
         + u * d_ref[...])
    z = _gelu_tanh(y)
    z = z * _sigmoid(jnp.dot(z.astype(BF16), wglu_ref[...], preferred_element_type=F32))
    z_ref[...] = z.reshape(tc, nb, width).astype(BF16)
    sr_out[...] = str_ref[...]
    si_out[...] = sti_ref[...]


def _ssm_branch(u_tm, s0r, s0i, ar, ai, bre, bim, cre, cim, d, wglu, tc):
    t, nb, width = u_tm.shape
    nst = ar.shape[1]
    lc = min(256, nst)
    kern = functools.partial(_ssm_kernel, tc=tc, nb=nb, lc=lc)
    return pl.pallas_call(
        kern,
        grid=(t // tc,),
        in_specs=[pl.BlockSpec((tc, nb, width), lambda c: (c, 0, 0)),
                  _const_spec((nb, nst)), _const_spec((nb, nst)),
                  _const_spec((1, nst)), _const_spec((1, nst)),
                  _const_spec((width, nst)), _const_spec((width, nst)),
                  _const_spec((nst, width)), _const_spec((nst, width)),
                  _const_spec((1, width)), _const_spec((width, width))],
        out_specs=[pl.BlockSpec((tc, nb, width), lambda c: (c, 0, 0)),
                   pl.BlockSpec((nb, nst), lambda c: (0, 0)), pl.BlockSpec((nb, nst), lambda c: (0, 0))],
        out_shape=[jax.ShapeDtypeStruct((t, nb, width), BF16),
                   jax.ShapeDtypeStruct((nb, nst), F32), jax.ShapeDtypeStruct((nb, nst), F32)],
        scratch_shapes=[pltpu.VMEM((tc * nb, nst), F32), pltpu.VMEM((tc * nb, nst), F32),
                        pltpu.VMEM((nb, nst), F32), pltpu.VMEM((nb, nst), F32)],
        compiler_params=_cparams("arbitrary"),
        name="ssm_branch",
    )(u_tm, s0r, s0i, ar, ai, bre, bim, cre, cim, d, wglu)


MERGE_CW = 512


def _merge_kernel(o_ref, z_ref, x_ref, g_ref, wg_ref, wa_ref, ws_ref, wo_ref, out_ref, mg_ref):
    x = x_ref[...]
    d = x.shape[1]
    h = _rms(x, g_ref[...]).astype(BF16)
    o = o_ref[...]
    z = z_ref[...]
    for c in range(d // MERGE_CW):
        csl = slice(c * MERGE_CW, (c + 1) * MERGE_CW)
        ga = jnp.dot(h, wg_ref[:, csl], preferred_element_type=F32)
        gs = jnp.dot(h, wg_ref[:, d + c * MERGE_CW:d + (c + 1) * MERGE_CW], preferred_element_type=F32)
        a = jnp.dot(o, wa_ref[:, csl], preferred_element_type=F32)
        s = jnp.dot(z, ws_ref[:, csl], preferred_element_type=F32)
        mg_ref[:, csl] = (_sigmoid(ga) * a + _sigmoid(gs) * s).astype(BF16)
    out_ref[...] = x + jnp.dot(mg_ref[...], wo_ref[...], preferred_element_type=F32)


def _merge(o_att, z, z_map, x, g, wg, wa, ws, wo, tm):
    n, d = x.shape
    zw = ws.shape[0]
    row = lambda i: (i, 0)
    return pl.pallas_call(
        _merge_kernel,
        grid=(n // tm,),
        in_specs=[pl.BlockSpec((tm, ATT_Q_W), row), pl.BlockSpec((tm, zw), z_map),
                  pl.BlockSpec((tm, d), row), _const_spec((1, d)), _const_spec(wg.shape),
                  _const_spec(wa.shape), _const_spec(ws.shape), _const_spec(wo.shape)],
        out_specs=pl.BlockSpec((tm, d), row),
        out_shape=jax.ShapeDtypeStruct((n, d), F32),
        scratch_shapes=[pltpu.VMEM((tm, d), BF16)],
        compiler_params=_cparams("arbitrary"),
        name="gated_merge",
    )(o_att, z, x, g, wg, wa, ws, wo)


def _ffn_kernel(x_ref, g_ref, init_ref, wup_ref, cw_ref, cb_ref, wdn_ref, out_ref, st_ref,
                h_ref, acc_ref, hist_ref, carry_ref, *, tm, tf, shift, tiles_per_seq):
    i = pl.program_id(0)
    f = pl.program_id(1)

    @pl.when(f == 0)
    def _():
        h_ref[...] = _rms(x_ref[...], g_ref[...]).astype(BF16)
        acc_ref[...] = jnp.zeros(acc_ref.shape, F32)

    first = (i % tiles_per_seq) == 0

    @pl.when(first)
    def _():
        hist_ref[...] = init_ref[...]

    @pl.when(jnp.logical_not(first))
    def _():
        hist_ref[...] = carry_ref[f]

    def conv_piece(csl):
        up = jnp.dot(h_ref[...], wup_ref[0, :, csl], preferred_element_type=F32)
        hist = hist_ref[:, csl]
        prev1 = jnp.concatenate([hist[shift:, :], up[:tm - shift, :]], axis=0)
        prev2 = jnp.concatenate([hist, up[:tm - 2 * shift, :]], axis=0)
        conv = (cw_ref[0, 2:3, csl] * up + cw_ref[0, 1:2, csl] * prev1 + cw_ref[0, 0:1, csl] * prev2
                + cb_ref[0, :, csl])
        return conv, up[tm - 2 * shift:, :]

    down = None
    for j in range(tf // FFN_SUB):
        gsl = slice(j * FFN_SUB, (j + 1) * FFN_SUB)
        vsl = slice(tf + j * FFN_SUB, tf + (j + 1) * FFN_SUB)
        gate, gate_tail = conv_piece(gsl)
        val, val_tail = conv_piece(vsl)
        for sl, tail in ((gsl, gate_tail), (vsl, val_tail)):
            carry_ref[f, :, sl] = tail
            st_ref[0, :, sl] = tail
        act = (_gelu_tanh(gate) * val).astype(BF16)
        part = jnp.dot(act, wdn_ref[0, j * FFN_SUB:(j + 1) * FFN_SUB, :], preferred_element_type=F32)
        down = part if down is None else down + part
    acc_ref[...] += down

    @pl.when(f == pl.num_programs(1) - 1)
    def _():
        out_ref[...] = x_ref[...] + acc_ref[...]


def _conv_ffn(x, g, init, wup, cw, cb, wdn, tm, shift, tiles_per_seq):
    n, d = x.shape
    nf, _, tf2 = wup.shape
    tf = tf2 // 2
    hist = 2 * shift
    kern = functools.partial(_ffn_kernel, tm=tm, tf=tf, shift=shift, tiles_per_seq=tiles_per_seq)
    return pl.pallas_call(
        kern,
        grid=(n // tm, nf),
        in_specs=[pl.BlockSpec((tm, d), lambda i, f: (i, 0)), _const_spec((1, d)),
                  pl.BlockSpec((hist, tf2), lambda i, f: (0, f)),
                  pl.BlockSpec((1, d, tf2), lambda i, f: (f, 0, 0)),
                  pl.BlockSpec((1, CONV_W, tf2), lambda i, f: (f, 0, 0)),
                  pl.BlockSpec((1, 1, tf2), lambda i, f: (f, 0, 0)),
                  pl.BlockSpec((1, tf, d), lambda i, f: (f, 0, 0))],
        out_specs=[pl.BlockSpec((tm, d), lambda i, f: (i, 0)),
                   pl.BlockSpec((1, hist, tf2), lambda i, f: (i, 0, f))],
        out_shape=[jax.ShapeDtypeStruct((n, d), F32),
                   jax.ShapeDtypeStruct((n // tm, hist, nf * tf2), F32)],
        scratch_shapes=[pltpu.VMEM((tm, d), BF16), pltpu.VMEM((tm, d), F32),
                        pltpu.VMEM((hist, tf2), F32), pltpu.VMEM((nf, hist, tf2), F32)],
        compiler_params=_cparams("arbitrary", "arbitrary"),
        name="conv_ffn",
    )(x, g, init, wup, cw, cb, wdn)


def _ple_kernel(x_ref, p_ref, gpg_ref, wpg_ref, wple_ref, gple_ref, out_ref):
    x = x_ref[...]
    ple = _rms(jnp.dot(p_ref[...].astype(BF16), wple_ref[...], preferred_element_type=F32), gple_ref[...])
    gate = _sigmoid(jnp.dot(_rms(x, gpg_ref[...]).astype(BF16), wpg_ref[...], preferred_element_type=F32))
    out_ref[...] = x + gate * ple


def _ple(x, p, gpg, wpg, wple, gple, tm):
    n, d = x.shape
    pd = p.shape[1]
    row = lambda i: (i, 0)
    return pl.pallas_call(
        _ple_kernel,
        grid=(n // tm,),
        in_specs=[pl.BlockSpec((tm, d), row), pl.BlockSpec((tm, pd), row), _const_spec((1, d)),
                  _const_spec((d, d)), _const_spec((pd, d)), _const_spec((1, d))],
        out_specs=pl.BlockSpec((tm, d), row),
        out_shape=jax.ShapeDtypeStruct((n, d), F32),
        compiler_params=_cparams("arbitrary"),
        name="ple_gate",
    )(x, p, gpg, wpg, wple, gple)


def _rope_tables(pos, head_dim, rot_dim):
    half = rot_dim // 2
    inv_freq = jnp.asarray(
        np.float32(ROPE_THETA) ** (-np.arange(half, dtype=np.float32) / np.float32(half)))
    ang = pos.astype(F32)[:, None] * inv_freq[None, :]
    cos, sin = jnp.cos(ang), jnp.sin(ang)
    n = pos.shape[0]
    ones = jnp.ones((n, head_dim - rot_dim), F32)
    zeros = jnp.zeros((n, head_dim - rot_dim), F32)
    zh = jnp.zeros((n, half), F32)
    c = jnp.concatenate([cos, cos, ones], axis=1)
    sa = jnp.concatenate([zh, sin, zeros], axis=1)
    sb = jnp.concatenate([-sin, zh, zeros], axis=1)
    reps = LANES // head_dim
    return jnp.stack([jnp.tile(t, (1, reps)) for t in (c, sa, sb)], axis=0)


def _ssm_params(lw):
    a_re = lw['ssm_a_re'].astype(F32)
    a_im = lw['ssm_a_im'].astype(F32)
    groups, nstate = a_re.shape
    dt = jnp.exp(lw['ssm_log_dt'].astype(F32))[:, None]
    mag = jnp.exp(a_re * dt)
    ang = a_im * dt
    ab_re, ab_im = mag * jnp.cos(ang), mag * jnp.sin(ang)
    den = a_re * a_re + a_im * a_im
    f_re = ((ab_re - 1.0) * a_re + ab_im * a_im) / den
    f_im = (ab_im * a_re - (ab_re - 1.0) * a_im) / den
    b_re = lw['ssm_b_re'].astype(F32)
    b_im = lw['ssm_b_im'].astype(F32)
    bb_re = f_re[..., None] * b_re - f_im[..., None] * b_im
    bb_im = f_re[..., None] * b_im + f_im[..., None] * b_re
    eye = jnp.eye(groups, dtype=F32)

    def in_map(bb):
        m = jnp.einsum('gnc,gh->gchn', bb, eye)
        return m.reshape(groups * SSM_GROUP, groups * nstate).astype(BF16)

    def out_map(cc):
        m = jnp.einsum('gcn,gh->gnhc', cc.astype(F32), eye)
        return m.reshape(groups * nstate, groups * SSM_GROUP).astype(BF16)

    return dict(ar=ab_re.reshape(1, -1), ai=ab_im.reshape(1, -1),
                bre=in_map(bb_re), bim=in_map(bb_im),
                cre=out_map(lw['ssm_c_re']), cim=out_map(lw['ssm_c_im']),
                d=lw['ssm_d'].astype(F32).reshape(1, -1), wglu=lw['w_glu'].astype(BF16))


def _chunk_features(a, d_ff, tf):
    nf = -(-d_ff // tf)
    pad = [(0, 0)] * (a.ndim - 1) + [(0, nf * tf - d_ff)]
    gate = jnp.pad(a[..., :d_ff], pad).reshape(a.shape[:-1] + (nf, tf))
    val = jnp.pad(a[..., d_ff:], pad).reshape(a.shape[:-1] + (nf, tf))
    return jnp.concatenate([gate, val], axis=-1)


def _unchunk_features(a, d_ff, tf):
    nf = a.shape[-1] // (2 * tf)
    a = a.reshape(a.shape[:-1] + (nf, 2, tf))
    gate = a[..., 0, :].reshape(a.shape[:-3] + (nf * tf,))[..., :d_ff]
    val = a[..., 1, :].reshape(a.shape[:-3] + (nf * tf,))[..., :d_ff]
    return jnp.concatenate([gate, val], axis=-1)


FFN_TF = 512
FFN_SUB = 256


def _layer_weights(lw):
    w_in = lw['w_in']
    d = w_in.shape[0]
    o = np.cumsum([0, ATT_Q_W, ATT_KV_W, ATT_KV_W, IDX_Q_W, IDX_DIM, IDX_HEADS])
    ssm_w = lw['ssm_d'].shape[0]
    w_q_kv = w_in[:, :o[3]]
    w_iq, w_ik, w_iw = w_in[:, o[3]:o[4]], w_in[:, o[4]:o[5]], w_in[:, o[5]:o[6]]
    w_u = w_in[:, o[6]:o[6] + ssm_w]
    w_gates = w_in[:, o[6] + ssm_w:]
    w_idx = jnp.concatenate([w_iq, w_ik, w_ik, w_iw, jnp.zeros((d, LANES - IDX_HEADS), w_in.dtype)], axis=1)
    d_ff = lw['w_down'].shape[0]
    nf = -(-d_ff // FFN_TF)
    wup = jnp.transpose(_chunk_features(lw['w_up'], d_ff, FFN_TF), (1, 0, 2)).astype(BF16)
    wdn = jnp.pad(lw['w_down'], ((0, nf * FFN_TF - d_ff), (0, 0))).reshape(nf, FFN_TF, d).astype(BF16)
    cw = jnp.transpose(_chunk_features(lw['conv_w'], d_ff, FFN_TF), (1, 0, 2)).astype(F32)
    cb = _chunk_features(lw['conv_b'], d_ff, FFN_TF).reshape(nf, 1, 2 * FFN_TF).astype(F32)
    out = dict(
        g_mix=lw['g_mix'].reshape(1, d), g_q=lw['g_q'].reshape(1, HEAD_DIM), g_k=lw['g_k'].reshape(1, HEAD_DIM),
        w_qkv=w_q_kv.astype(BF16), w_idx=w_idx.astype(BF16), w_u=w_u.astype(BF16),
        w_gates=w_gates.astype(BF16),
        w_att_br=lw['w_att_br'].astype(BF16), w_ssm_br=lw['w_ssm_br'].astype(BF16), w_o=lw['w_o'].astype(BF16),
        g_ffn=lw['g_ffn'].reshape(1, d), wup=wup, wdn=wdn, cw=cw, cb=cb, d_ff=d_ff,
        g_pg=lw['g_pg'].reshape(1, d), w_pg=lw['w_pg'].astype(BF16), w_ple=lw['w_ple'].astype(BF16),
        g_ple=lw['g_ple'].reshape(1, d))
    out.update(_ssm_params(lw))
    return out


def _row_tile(n, cap):
    return cap if n % cap == 0 else n


def _prompt_layer(x3, p3, w):
    batch, seq, d = x3.shape
    n = batch * seq
    x = x3.reshape(n, d)
    tm = _row_tile(seq, 512)
    tpb = seq // tm
    pos = jnp.arange(seq)
    tab_att = _rope_tables(pos, HEAD_DIM, ROT_DIM)
    tab_idx = _rope_tables(pos, IDX_DIM, IDX_ROT_DIM)

    q, k, v, kb, vb = _qkv_proj(x, w['g_mix'], w['w_qkv'], w['g_q'], w['g_k'], tab_att, tm)
    iq, ik2, ikb, iw = _idx_proj(x, w['g_mix'], w['w_idx'], tab_idx, tm)
    ssm_w = w['w_u'].shape[1]
    u_tm = _norm_matmul(x, w['g_mix'], w['w_u'], tm, ssm_w,
                        jax.ShapeDtypeStruct((seq, batch * ssm_w), F32),
                        lambda i, j: (i % tpb, i // tpb), "u_proj").reshape(seq, batch, ssm_w)

    o_att = _prompt_attention(q, iq, iw, ikb, kb, vb, batch, seq)

    nst = w['ar'].shape[1]
    zeros = jnp.zeros((batch, nst), F32)
    z_tm, s_re, s_im = _ssm_branch(u_tm, zeros, zeros, w['ar'], w['ai'], w['bre'], w['bim'],
                                   w['cre'], w['cim'], w['d'], w['wglu'], tc=min(32, seq))
    tmm = _row_tile(seq, 256)
    tpbm = seq // tmm
    x1 = _merge(o_att, z_tm.reshape(seq, batch * ssm_w), lambda i: (i % tpbm, i // tpbm),
                x, w['g_mix'], w['w_gates'], w['w_att_br'], w['w_ssm_br'], w['w_o'], tmm)

    nf, _, tf2 = w['wup'].shape
    x2, conv = _conv_ffn(x1, w['g_ffn'], jnp.zeros((CONV_W - 1, nf * tf2), F32),
                         w['wup'], w['cw'], w['cb'], w['wdn'], tm, 1, tpb)
    conv = _unchunk_features(conv[tpb - 1::tpb], w['d_ff'], tf2 // 2)
    y = _ple(x2, p3.reshape(n, -1), w['g_pg'], w['w_pg'], w['w_ple'], w['g_ple'], tm)

    groups = nst // SSM_STATE
    return (y.reshape(batch, seq, d), k.reshape(batch, seq, KV_HEADS, HEAD_DIM),
            v.reshape(batch, seq, KV_HEADS, HEAD_DIM), ik2[:, :IDX_DIM].reshape(batch, seq, IDX_DIM),
            s_re.reshape(batch, groups, SSM_STATE), s_im.reshape(batch, groups, SSM_STATE), conv)


def _pad_rows(a, rows):
    pad = [(0, 0)] * a.ndim
    pad[1] = (0, rows - a.shape[1])
    return jnp.pad(a, pad)


def _sample_layer(x3, p3, cache_k, cache_v, cache_ik, s0_re, s0_im, conv0, page_table, w):
    batch, t_new, d = x3.shape
    n = batch * t_new
    n_pages = page_table.shape[1]
    past = n_pages * PAGE_SIZE
    x = x3.reshape(n, d)
    pos = past + (jnp.arange(n) % t_new)
    tab_att = _rope_tables(pos, HEAD_DIM, ROT_DIM)
    tab_idx = _rope_tables(pos, IDX_DIM, IDX_ROT_DIM)

    q, k, v, kb, vb = _qkv_proj(x, w['g_mix'], w['w_qkv'], w['g_q'], w['g_k'], tab_att, n)
    iq, ik2, ikb, iw = _idx_proj(x, w['g_mix'], w['w_idx'], tab_idx, n)
    ssm_w = w['w_u'].shape[1]
    u = _norm_matmul(x, w['g_mix'], w['w_u'], n, ssm_w,
                     jax.ShapeDtypeStruct((n, ssm_w), F32), lambda i, j: (i, j), "u_proj_s")

    per_seq = lambda a: a.reshape(batch, t_new, a.shape[-1])
    pages = 8 if n_pages % 8 == 0 else 1
    sc, thr, scn = _sample_scores(
        page_table, _pad_rows(per_seq(iq), SUBLANES), _pad_rows(per_seq(iw), SUBLANES),
        jnp.swapaxes(_pad_rows(per_seq(ikb[:, :IDX_DIM]), LANES), 1, 2), cache_ik, pages, t_new)
    new_page = lambda a: _pad_rows(a.reshape(batch, t_new * KV_HEADS, HEAD_DIM), PAGE_SIZE * KV_HEADS)
    o_att = _sample_attention(
        page_table, _pad_rows(per_seq(q), SUBLANES), sc, thr, scn,
        new_page(k), new_page(v), cache_k, cache_v, pages)
    o_att = o_att[:, :t_new].reshape(n, ATT_Q_W)

    nst = w['ar'].shape[1]
    u_tm = jnp.transpose(u.reshape(batch, t_new, ssm_w), (1, 0, 2))
    z_tm, s_re, s_im = _ssm_branch(u_tm, s0_re.reshape(batch, nst).astype(F32),
                                   s0_im.reshape(batch, nst).astype(F32), w['ar'], w['ai'], w['bre'],
                                   w['bim'], w['cre'], w['cim'], w['d'], w['wglu'], tc=t_new)
    z = jnp.transpose(z_tm, (1, 0, 2)).reshape(n, ssm_w)
    x1 = _merge(o_att, z, lambda i: (i, 0), x, w['g_mix'], w['w_gates'], w['w_att_br'], w['w_ssm_br'],
                w['w_o'], n)

    nf, _, tf2 = w['wup'].shape
    d_ff = w['d_ff']
    x1_tm = jnp.transpose(x1.reshape(batch, t_new, d), (1, 0, 2)).reshape(n, d)
    init = _chunk_features(jnp.transpose(conv0.astype(F32), (1, 0, 2)), d_ff, tf2 // 2)
    init = init.reshape((CONV_W - 1) * batch, nf * tf2)
    x2_tm, conv = _conv_ffn(x1_tm, w['g_ffn'], init, w['wup'], w['cw'], w['cb'], w['wdn'], n, batch, 1)
    conv = _unchunk_features(conv.reshape(CONV_W - 1, batch, nf * tf2), d_ff, tf2 // 2)
    conv = jnp.transpose(conv, (1, 0, 2))
    x2 = jnp.transpose(x2_tm.reshape(t_new, batch, d), (1, 0, 2)).reshape(n, d)
    y = _ple(x2, p3.reshape(n, -1), w['g_pg'], w['w_pg'], w['w_ple'], w['g_ple'], n)

    groups = nst // SSM_STATE
    return (y.reshape(batch, t_new, d), k.reshape(batch, t_new, KV_HEADS, HEAD_DIM),
            v.reshape(batch, t_new, KV_HEADS, HEAD_DIM), ik2[:, :IDX_DIM].reshape(batch, t_new, IDX_DIM),
            s_re.reshape(batch, groups, SSM_STATE), s_im.reshape(batch, groups, SSM_STATE), conv)


def kernel(x_prompt, x_sample, cache_k, cache_v, cache_idx_k, state_ssm_re, state_ssm_im, state_conv, page_table, p_prompt, p_sample, g_mix, w_in, g_q, g_k, ssm_a_re, ssm_a_im, ssm_log_dt, ssm_b_re, ssm_b_im, ssm_c_re, ssm_c_im, ssm_d, w_glu, w_att_br, w_ssm_br, w_o, g_ffn, w_up, conv_w, conv_b, w_down, w_ple, g_ple, g_pg, w_pg):
    depth = w_in.shape[0]
    n_pool = cache_k.shape[1]
    pool_k = cache_k.reshape(depth * n_pool, PAGE_SIZE * KV_HEADS, HEAD_DIM)
    pool_v = cache_v.reshape(depth * n_pool, PAGE_SIZE * KV_HEADS, HEAD_DIM)
    pool_ik = jnp.swapaxes(cache_idx_k, 2, 3).reshape(depth * n_pool, IDX_DIM, PAGE_SIZE)
    xp, xs = x_prompt, x_sample
    outs_p, outs_s = [], []
    for i in range(depth):
        lw = dict(g_mix=g_mix[i], w_in=w_in[i], g_q=g_q[i], g_k=g_k[i],
                  ssm_a_re=ssm_a_re[i], ssm_a_im=ssm_a_im[i], ssm_log_dt=ssm_log_dt[i],
                  ssm_b_re=ssm_b_re[i], ssm_b_im=ssm_b_im[i], ssm_c_re=ssm_c_re[i], ssm_c_im=ssm_c_im[i],
                  ssm_d=ssm_d[i], w_glu=w_glu[i], w_att_br=w_att_br[i], w_ssm_br=w_ssm_br[i], w_o=w_o[i],
                  g_ffn=g_ffn[i], w_up=w_up[i], conv_w=conv_w[i], conv_b=conv_b[i], w_down=w_down[i],
                  w_ple=w_ple[i], g_ple=g_ple[i], g_pg=g_pg[i], w_pg=w_pg[i])
        w = _layer_weights(lw)
        xp, *rest_p = _prompt_layer(xp, p_prompt[i], w)
        outs_p.append(rest_p)
        xs, *rest_s = _sample_layer(xs, p_sample[i], pool_k, pool_v, pool_ik, state_ssm_re[i],
                                    state_ssm_im[i], state_conv[i], page_table + i * n_pool, w)
        outs_s.append(rest_s)
    stack = lambda outs, j: jnp.stack([o[j] for o in outs], 0)
    return ((xp, xs) + tuple(stack(outs_p, j) for j in range(6))
            + tuple(stack(outs_s, j) for j in range(6)))
```

```python
import functools

import numpy as np
import jax
import jax.numpy as jnp
from jax import lax
from jax.experimental import pallas as pl
from jax.experimental.pallas import tpu as pltpu

F32 = jnp.float32
BF16 = jnp.bfloat16
I32 = jnp.int32

ATT_HEADS = 8
KV_HEADS = 2
HEAD_DIM = 128
GROUP = ATT_HEADS // KV_HEADS
ROT_DIM = HEAD_DIM // 4
ROPE_THETA = 500000.0
IDX_HEADS = 16
IDX_DIM = 64
IDX_ROT_DIM = IDX_DIM // 4
IDX_SCALE = (IDX_HEADS ** -0.5) * (IDX_DIM ** -0.5)
TOPK_MAX = 256
Q_BLOCK = 128
PAGE_SIZE = 128
SSM_GROUP = 16
SSM_STATE = 64
CONV_W = 3
EPS = 1e-6

LANES = 128
SUBLANES = 8
NEG = -1e30
INT_MIN = -2 ** 31
VMEM_LIMIT = 56 * 1024 * 1024

ATT_Q_W = ATT_HEADS * HEAD_DIM
ATT_KV_W = KV_HEADS * HEAD_DIM
IDX_Q_W = IDX_HEADS * IDX_DIM


def _cparams(*sem):
    return pltpu.CompilerParams(dimension_semantics=sem, vmem_limit_bytes=VMEM_LIMIT)


def _const_spec(shape):
    nd = len(shape)
    return pl.BlockSpec(shape, lambda *_: (0,) * nd, pipeline_mode=pl.Buffered(1))


def _rms(x, g):
    ms = jnp.mean(x * x, axis=-1, keepdims=True)
    return (x * lax.rsqrt(ms + EPS)) * g


def _gelu_tanh(x):
    return 0.5 * x * (1.0 + jnp.tanh(0.7978845608028654 * (x + 0.044715 * (x * x * x))))


def _sigmoid(x):
    return 1.0 / (1.0 + jnp.exp(-x))


def _rope(x, tab_ref, half):
    return (x * tab_ref[0] + pltpu.roll(x, half, 1) * tab_ref[1]
            + pltpu.roll(x, LANES - half, 1) * tab_ref[2])


def _dot_t(a, b):
    return lax.dot_general(a, b, (((1,), (1,)), ((), ())), preferred_element_type=F32)


def _qkv_kernel(x_ref, g_ref, w_ref, gq_ref, gk_ref, tab_ref, q_ref, k_ref, v_ref, kb_ref, vb_ref):
    h = _rms(x_ref[...], g_ref[...]).astype(BF16)
    y = jnp.dot(h, w_ref[...], preferred_element_type=F32)
    half = ROT_DIM // 2
    for hh in range(ATT_HEADS):
        sl = slice(hh * HEAD_DIM, (hh + 1) * HEAD_DIM)
        qh = _rope(_rms(y[:, sl], gq_ref[...]), tab_ref, half)
        q_ref[:, sl] = (qh * (HEAD_DIM ** -0.5)).astype(BF16)
    for hh in range(KV_HEADS):
        sl = slice(hh * HEAD_DIM, (hh + 1) * HEAD_DIM)
        kh = _rope(_rms(y[:, ATT_Q_W + hh * HEAD_DIM:ATT_Q_W + (hh + 1) * HEAD_DIM], gk_ref[...]),
                   tab_ref, half)
        k_ref[:, sl] = kh
        kb_ref[:, sl] = kh.astype(BF16)
    vv = y[:, ATT_Q_W + ATT_KV_W:]
    v_ref[...] = vv
    vb_ref[...] = vv.astype(BF16)


def _qkv_proj(x, g, w, gq, gk, tab, tm):
    n, d = x.shape
    wd = w.shape[1]
    p_blocks = tab.shape[1] // tm
    row = lambda i: (i, 0)
    return pl.pallas_call(
        _qkv_kernel,
        grid=(n // tm,),
        in_specs=[pl.BlockSpec((tm, d), row), _const_spec((1, d)), _const_spec((d, wd)),
                  _const_spec((1, HEAD_DIM)), _const_spec((1, HEAD_DIM)),
                  pl.BlockSpec((3, tm, LANES), lambda i: (0, i % p_blocks, 0))],
        out_specs=[pl.BlockSpec((tm, ATT_Q_W), row), pl.BlockSpec((tm, ATT_KV_W), row),
                   pl.BlockSpec((tm, ATT_KV_W), row), pl.BlockSpec((tm, ATT_KV_W), row),
                   pl.BlockSpec((tm, ATT_KV_W), row)],
        out_shape=[jax.ShapeDtypeStruct((n, ATT_Q_W), BF16), jax.ShapeDtypeStruct((n, ATT_KV_W), F32),
                   jax.ShapeDtypeStruct((n, ATT_KV_W), F32), jax.ShapeDtypeStruct((n, ATT_KV_W), BF16),
                   jax.ShapeDtypeStruct((n, ATT_KV_W), BF16)],
        compiler_params=_cparams("arbitrary"),
        name="qkv_proj",
    )(x, g, w, gq, gk, tab)


def _idx_kernel(x_ref, g_ref, w_ref, tab_ref, iq_ref, ik_ref, ikb_ref, iw_ref):
    h = _rms(x_ref[...], g_ref[...]).astype(BF16)
    y = jnp.dot(h, w_ref[...], preferred_element_type=F32)
    half = IDX_ROT_DIM // 2
    for s in range(IDX_Q_W // LANES):
        sl = slice(s * LANES, (s + 1) * LANES)
        iq_ref[:, sl] = _rope(y[:, sl], tab_ref, half).astype(BF16)
    ik2 = _rope(y[:, IDX_Q_W:IDX_Q_W + LANES], tab_ref, half)
    ik_ref[...] = ik2
    ikb_ref[...] = ik2.astype(BF16)
    iw_ref[...] = y[:, IDX_Q_W + LANES:] * IDX_SCALE


def _idx_proj(x, g, w, tab, tm):
    n, d = x.shape
    wd = w.shape[1]
    p_blocks = tab.shape[1] // tm
    row = lambda i: (i, 0)
    return pl.pallas_call(
        _idx_kernel,
        grid=(n // tm,),
        in_specs=[pl.BlockSpec((tm, d), row), _const_spec((1, d)), _const_spec((d, wd)),
                  pl.BlockSpec((3, tm, LANES), lambda i: (0, i % p_blocks, 0))],
        out_specs=[pl.BlockSpec((tm, IDX_Q_W), row), pl.BlockSpec((tm, LANES), row),
                   pl.BlockSpec((tm, LANES), row), pl.BlockSpec((tm, LANES), row)],
        out_shape=[jax.ShapeDtypeStruct((n, IDX_Q_W), BF16), jax.ShapeDtypeStruct((n, LANES), F32),
                   jax.ShapeDtypeStruct((n, LANES), BF16), jax.ShapeDtypeStruct((n, LANES), F32)],
        compiler_params=_cparams("arbitrary"),
        name="idx_proj",
    )(x, g, w, tab)


def _norm_mm_kernel(x_ref, g_ref, w_ref, o_ref, h_ref):
    @pl.when(pl.program_id(1) == 0)
    def _():
        h_ref[...] = _rms(x_ref[...], g_ref[...]).astype(BF16)
    o_ref[...] = jnp.dot(h_ref[...], w_ref[...], preferred_element_type=F32).astype(o_ref.dtype)


def _norm_matmul(x, g, w, tm, tn, out_shape, out_map, name):
    n, d = x.shape
    return pl.pallas_call(
        _norm_mm_kernel,
        grid=(n // tm, w.shape[1] // tn),
        in_specs=[pl.BlockSpec((tm, d), lambda i, j: (i, 0)), _const_spec((1, d)),
                  pl.BlockSpec((d, tn), lambda i, j: (0, j))],
        out_specs=pl.BlockSpec((tm, tn), out_map),
        out_shape=out_shape,
        scratch_shapes=[pltpu.VMEM((tm, d), BF16)],
        compiler_params=_cparams("arbitrary", "arbitrary"),
        name=name,
    )(x, g, w)


def _sort_key(sc):
    bits = pltpu.bitcast(sc, I32)
    return bits ^ ((bits >> 31) & jnp.int32(0x7FFFFFFF))


def _kth_largest(count_ge, shape, topk):
    def body(it, t):
        cand = t + lax.shift_left(jnp.int32(1), 31 - it)
        return jnp.where(count_ge(cand) >= topk, cand, t)
    t = lax.fori_loop(0, 32, body, jnp.full(shape, INT_MIN, I32))
    return jnp.maximum(t, INT_MIN + 1)


def _tree(op, parts):
    while len(parts) > 1:
        nxt = [op(parts[j], parts[j + 1]) for j in range(0, len(parts) - 1, 2)]
        if len(parts) % 2:
            nxt.append(parts[-1])
        parts = nxt
    return parts[0]


def _row_groups(op, x):
    return _tree(op, [x[r:r + SUBLANES, :] for r in range(0, x.shape[0], SUBLANES)])


def _indexer_operands(iq, iw, lhs_ref, wb_ref, rows):
    for h in range(IDX_HEADS):
        lhs_ref[h * rows:(h + 1) * rows, :] = iq[:, h * IDX_DIM:(h + 1) * IDX_DIM].astype(F32)
        wb_ref[h * rows:(h + 1) * rows, :] = jnp.broadcast_to(iw[:, h:h + 1], (rows, LANES))


def _indexer_scores(lhs_ref, wb_ref, keys_t, rows):
    n_keys = keys_t.shape[1]
    r = jnp.dot(lhs_ref[...].astype(BF16), keys_t, preferred_element_type=F32)
    sc = jnp.zeros((rows, n_keys), F32)
    for h in range(IDX_HEADS):
        wh = jnp.concatenate([wb_ref[h * rows:(h + 1) * rows, :]] * (n_keys // LANES), axis=1)
        sc = sc + wh * jnp.maximum(r[h * rows:(h + 1) * rows, :], 0.0)
    return sc


def _prompt_attn_kernel(q_ref, iq_ref, iw_ref, ikb_ref, kb_ref, vb_ref, o_ref,
                        lhs_ref, q2_ref, vt_ref, key_ref, s_ref, m_ref, l_ref, acc_ref, *, seq, kc, topk):
    qb = Q_BLOCK
    npair = ATT_HEADS // 2
    i = pl.program_id(1)
    t0 = i * qb
    nch = (t0 + qb + kc - 1) // kc

    @pl.when(i == 0)
    def _():
        def transpose_chunk(c, carry):
            c0 = pl.multiple_of(c * kc, kc)
            vt_ref[c] = vb_ref[pl.ds(c0, kc), :].astype(F32).T.astype(BF16)
            return carry
        lax.fori_loop(0, seq // kc, transpose_chunk, 0)

    lane = lax.broadcasted_iota(I32, (qb, LANES), 1)
    for s in range(IDX_Q_W // LANES):
        x = iq_ref[:, s * LANES:(s + 1) * LANES].astype(F32)
        lhs_ref[(2 * s) * qb:(2 * s + 1) * qb, :] = jnp.where(lane < IDX_DIM, x, 0.0).astype(BF16)
        lhs_ref[(2 * s + 1) * qb:(2 * s + 2) * qb, :] = jnp.where(lane >= IDX_DIM, x, 0.0).astype(BF16)
    iwt = iw_ref[...].T
    for p in range(npair):
        for r in range(2):
            hsl = slice((2 * p + r) * HEAD_DIM, (2 * p + r + 1) * HEAD_DIM)
            q2_ref[p, r * qb:(r + 1) * qb, :] = q_ref[:, hsl]

    key_pos = lax.broadcasted_iota(I32, (kc, LANES), 0)
    q_pos = t0 + lax.broadcasted_iota(I32, (kc, LANES), 1)

    def score_chunk(c, carry):
        c0 = pl.multiple_of(c * kc, kc)
        ikc = ikb_ref[pl.ds(c0, kc), :]
        sc = jnp.zeros((kc, LANES), F32)
        for s in range(IDX_HEADS // 2):
            r = _dot_t(ikc, lhs_ref[2 * s * qb:(2 * s + 2) * qb, :])
            sc = (sc + iwt[2 * s:2 * s + 1, :] * jnp.maximum(r[:, :LANES], 0.0)
                  + iwt[2 * s + 1:2 * s + 2, :] * jnp.maximum(r[:, LANES:], 0.0))
        key_ref[c] = jnp.where(c0 + key_pos <= q_pos, _sort_key(sc), INT_MIN)
        return carry
    lax.fori_loop(0, nch, score_chunk, 0)

    def count_ge(cand):
        def chunk(c, acc):
            return acc + _row_groups(jnp.add, jnp.where(key_ref[c] >= cand, 1.0, 0.0))
        acc = lax.fori_loop(0, nch, chunk, jnp.zeros((SUBLANES, LANES), F32))
        return jnp.sum(acc, axis=0, keepdims=True)
    thr = _kth_largest(count_ge, (1, LANES), float(topk))

    m_ref[...] = jnp.full(m_ref.shape, NEG, F32)
    l_ref[...] = jnp.zeros(l_ref.shape, F32)
    acc_ref[...] = jnp.zeros(acc_ref.shape, F32)

    def score_pass(c, carry):
        c0 = pl.multiple_of(c * kc, kc)
        sel = key_ref[c] >= thr
        for p in range(npair):
            gsl = slice((2 * p) // GROUP * HEAD_DIM, ((2 * p) // GROUP + 1) * HEAD_DIM)
            s = _dot_t(kb_ref[pl.ds(c0, kc), gsl], q2_ref[p])
            s = jnp.concatenate([jnp.where(sel, s[:, :LANES], NEG), jnp.where(sel, s[:, LANES:], NEG)], axis=1)
            s_ref[c, p] = s
            m_ref[p] = jnp.maximum(m_ref[p], _row_groups(jnp.maximum, s))
        return carry
    lax.fori_loop(0, nch, score_pass, 0)
    m = [jnp.max(m_ref[p], axis=0, keepdims=True) for p in range(npair)]

    def value_pass(c, carry):
        for p in range(npair):
            gsl = slice((2 * p) // GROUP * HEAD_DIM, ((2 * p) // GROUP + 1) * HEAD_DIM)
            pr = jnp.exp(s_ref[c, p] - m[p])
            l_ref[p] += _row_groups(jnp.add, pr)
            acc_ref[p] += jnp.dot(vt_ref[c, gsl, :], pr.astype(BF16),
                                  preferred_element_type=F32)
        return carry
    lax.fori_loop(0, nch, value_pass, 0)

    for p in range(npair):
        o_t = acc_ref[p] / jnp.sum(l_ref[p], axis=0, keepdims=True)
        for r in range(2):
            hsl = slice((2 * p + r) * HEAD_DIM, (2 * p + r + 1) * HEAD_DIM)
            o_ref[:, hsl] = o_t[:, r * qb:(r + 1) * qb].T.astype(BF16)


def _prompt_attention(q, iq, iw, ikb, kb, vb, batch, seq):
    kc = min(256, seq)
    topk = min(TOPK_MAX, seq // 4)
    nq = seq // Q_BLOCK
    blk = lambda b, i: (b * nq + i, 0)
    full = lambda b, i: (b, 0)
    kern = functools.partial(_prompt_attn_kernel, seq=seq, kc=kc, topk=topk)
    return pl.pallas_call(
        kern,
        grid=(batch, nq),
        in_specs=[pl.BlockSpec((Q_BLOCK, ATT_Q_W), blk), pl.BlockSpec((Q_BLOCK, IDX_Q_W), blk),
                  pl.BlockSpec((Q_BLOCK, LANES), blk), pl.BlockSpec((seq, LANES), full),
                  pl.BlockSpec((seq, ATT_KV_W), full), pl.BlockSpec((seq, ATT_KV_W), full)],
        out_specs=pl.BlockSpec((Q_BLOCK, ATT_Q_W), blk),
        out_shape=jax.ShapeDtypeStruct((batch * seq, ATT_Q_W), BF16),
        scratch_shapes=[pltpu.VMEM((IDX_HEADS * Q_BLOCK, LANES), BF16),
                        pltpu.VMEM((ATT_HEADS // 2, 2 * Q_BLOCK, HEAD_DIM), BF16),
                        pltpu.VMEM((seq // kc, ATT_KV_W, kc), BF16),
                        pltpu.VMEM((seq // kc, kc, Q_BLOCK), I32),
                        pltpu.VMEM((seq // kc, ATT_HEADS // 2, kc, 2 * Q_BLOCK), F32),
                        pltpu.VMEM((ATT_HEADS // 2, SUBLANES, 2 * Q_BLOCK), F32),
                        pltpu.VMEM((ATT_HEADS // 2, SUBLANES, 2 * Q_BLOCK), F32),
                        pltpu.VMEM((ATT_HEADS // 2, HEAD_DIM, 2 * Q_BLOCK), F32)],
        compiler_params=_cparams("arbitrary", "arbitrary"),
        name="prompt_attention",
    )(q, iq, iw, ikb, kb, vb)


def _sample_score_kernel(pt_ref, iq_ref, iw_ref, ikn_ref, *rest, pages, n_new):
    page_refs = rest[:pages]
    sc_ref, thr_ref, scn_ref, lhs_ref, wb_ref, key_ref = rest[pages:]
    del pt_ref
    c = pl.program_id(1)
    nchunks = pl.num_programs(1)
    rows = SUBLANES

    @pl.when(c == 0)
    def _():
        _indexer_operands(iq_ref[0], iw_ref[0], lhs_ref, wb_ref, rows)

    keys_t = jnp.concatenate([page_refs[p][0] for p in range(pages)], axis=1).astype(BF16)
    sc = _indexer_scores(lhs_ref, wb_ref, keys_t, rows)
    sc_ref[0] = sc
    key = _sort_key(sc)
    for p in range(pages):
        key_ref[c * pages + p] = key[:, p * PAGE_SIZE:(p + 1) * PAGE_SIZE]

    @pl.when(c == nchunks - 1)
    def _():
        scn = _indexer_scores(lhs_ref, wb_ref, ikn_ref[0], rows)
        scn_ref[0] = scn
        t = lax.broadcasted_iota(I32, (rows, LANES), 0)
        j = lax.broadcasted_iota(I32, (rows, LANES), 1)
        keyn = jnp.where(j <= t, _sort_key(scn), INT_MIN)
        total = key_ref.shape[0]

        def count_ge(cand):
            hit = _tree(jnp.add, [jnp.where(key_ref[pg] >= cand, 1.0, 0.0) for pg in range(total)])
            return jnp.sum(hit + jnp.where(keyn >= cand, 1.0, 0.0), axis=1, keepdims=True)
        topk = float(min(TOPK_MAX, (total * PAGE_SIZE + n_new) // 4))
        thr = _kth_largest(count_ge, (rows, 1), topk)
        thr_ref[0] = jnp.broadcast_to(thr, (rows, LANES))


def _sample_scores(page_table, iq, iw, ikb_new, cache_ik, pages, n_new):
    batch, n_pages = page_table.shape
    rows = SUBLANES
    past = n_pages * PAGE_SIZE
    nchunks = n_pages // pages
    seq = lambda b, c, pt: (b, 0, 0)
    page_specs = [pl.BlockSpec((1, IDX_DIM, PAGE_SIZE),
                               functools.partial(lambda b, c, pt, p: (pt[b, c * pages + p], 0, 0), p=p))
                  for p in range(pages)]
    kern = functools.partial(_sample_score_kernel, pages=pages, n_new=n_new)
    grid_spec = pltpu.PrefetchScalarGridSpec(
        num_scalar_prefetch=1,
        grid=(batch, nchunks),
        in_specs=[pl.BlockSpec((1, rows, IDX_Q_W), seq), pl.BlockSpec((1, rows, LANES), seq),
                  pl.BlockSpec((1, IDX_DIM, LANES), seq)] + page_specs,
        out_specs=[pl.BlockSpec((1, rows, pages * PAGE_SIZE), lambda b, c, pt: (b, 0, c)),
                   pl.BlockSpec((1, rows, LANES), seq), pl.BlockSpec((1, rows, LANES), seq)],
        scratch_shapes=[pltpu.VMEM((IDX_HEADS * rows, IDX_DIM), F32),
                        pltpu.VMEM((IDX_HEADS * rows, LANES), F32),
                        pltpu.VMEM((n_pages, rows, PAGE_SIZE), I32)],
    )
    return pl.pallas_call(
        kern,
        grid_spec=grid_spec,
        out_shape=[jax.ShapeDtypeStruct((batch, rows, past), F32),
                   jax.ShapeDtypeStruct((batch, rows, LANES), I32),
                   jax.ShapeDtypeStruct((batch, rows, LANES), F32)],
        compiler_params=_cparams("arbitrary", "arbitrary"),
        name="sample_scores",
    )(page_table, iq, iw, ikb_new, *([cache_ik] * pages))


def _sample_attn_kernel(pt_ref, q_ref, sc_ref, thr_ref, scn_ref, kn_ref, vn_ref, *rest, pages):
    k_refs = rest[:pages]
    v_refs = rest[pages:2 * pages]
    o_ref, qa_ref, m_ref, l_ref, acc_ref = rest[2 * pages:]
    del pt_ref
    c = pl.program_id(1)
    nchunks = pl.num_programs(1)
    rows = SUBLANES
    pcols = PAGE_SIZE * KV_HEADS
    thr = thr_ref[0][:, 0:1]

    @pl.when(c == 0)
    def _():
        for h in range(ATT_HEADS):
            qa_ref[h * rows:(h + 1) * rows, :] = q_ref[0, :, h * HEAD_DIM:(h + 1) * HEAD_DIM].astype(F32)
        m_ref[...] = jnp.full(m_ref.shape, NEG, F32)
        l_ref[...] = jnp.zeros(l_ref.shape, F32)
        acc_ref[...] = jnp.zeros(acc_ref.shape, F32)

    kk = lax.broadcasted_iota(I32, (PAGE_SIZE, KV_HEADS * pcols), 0)
    cc = lax.broadcasted_iota(I32, (PAGE_SIZE, KV_HEADS * pcols), 1)
    head_bits = KV_HEADS.bit_length() - 1
    col = cc & (pcols - 1)
    expand = jnp.where(((col >> head_bits) == kk) & ((col & (KV_HEADS - 1)) == (cc >> (pcols.bit_length() - 1))),
                       1.0, 0.0).astype(BF16)

    def update(sel, kchunk, vchunk):
        n = sel.shape[1] // PAGE_SIZE
        stack = jnp.concatenate([sel[:, p * PAGE_SIZE:(p + 1) * PAGE_SIZE] for p in range(n)], axis=0)
        e = jnp.dot(stack.astype(BF16), expand, preferred_element_type=F32)
        mask = jnp.concatenate(
            [jnp.concatenate([jnp.concatenate(
                [e[p * rows:(p + 1) * rows, g * pcols:(g + 1) * pcols] for p in range(n)], axis=1)] * GROUP, axis=0)
             for g in range(KV_HEADS)], axis=0)
        s = jnp.where(mask > 0.5, _dot_t(qa_ref[...].astype(BF16), kchunk), NEG)
        m_old = m_ref[...]
        m_new = jnp.maximum(m_old, jnp.max(s, axis=1, keepdims=True))
        alpha = jnp.exp(m_old - m_new)
        p = jnp.exp(s - m_new)
        l_ref[...] = alpha * l_ref[...] + jnp.sum(p, axis=1, keepdims=True)
        acc_ref[...] = alpha * acc_ref[...] + jnp.dot(p.astype(BF16), vchunk, preferred_element_type=F32)
        m_ref[...] = m_new

    sel = jnp.where(_sort_key(sc_ref[0]) >= thr, 1.0, 0.0)
    update(sel, jnp.concatenate([k_refs[p][0] for p in range(pages)], axis=0).astype(BF16),
           jnp.concatenate([v_refs[p][0] for p in range(pages)], axis=0).astype(BF16))

    @pl.when(c == nchunks - 1)
    def _():
        t = lax.broadcasted_iota(I32, (rows, LANES), 0)
        j = lax.broadcasted_iota(I32, (rows, LANES), 1)
        seln = jnp.where(jnp.where(j <= t, _sort_key(scn_ref[0]), INT_MIN) >= thr, 1.0, 0.0)
        update(seln, kn_ref[0].astype(BF16), vn_ref[0].astype(BF16))
        out = acc_ref[...] / l_ref[...]
        for h in range(ATT_HEADS):
            o_ref[0, :, h * HEAD_DIM:(h + 1) * HEAD_DIM] = out[h * rows:(h + 1) * rows, :].astype(BF16)


def _sample_attention(page_table, q, sc, thr, scn, kb_new, vb_new, cache_k, cache_v, pages):
    batch, n_pages = page_table.shape
    rows = SUBLANES
    nchunks = n_pages // pages
    seq = lambda b, c, pt: (b, 0, 0)
    page_map = lambda p: functools.partial(lambda b, c, pt, p: (pt[b, c * pages + p], 0, 0), p=p)
    pcols = PAGE_SIZE * KV_HEADS
    k_specs = [pl.BlockSpec((1, pcols, HEAD_DIM), page_map(p)) for p in range(pages)]
    v_specs = [pl.BlockSpec((1, pcols, HEAD_DIM), page_map(p)) for p in range(pages)]
    kern = functools.partial(_sample_attn_kernel, pages=pages)
    grid_spec = pltpu.PrefetchScalarGridSpec(
        num_scalar_prefetch=1,
        grid=(batch, nchunks),
        in_specs=[pl.BlockSpec((1, rows, ATT_Q_W), seq),
                  pl.BlockSpec((1, rows, pages * PAGE_SIZE), lambda b, c, pt: (b, 0, c)),
                  pl.BlockSpec((1, rows, LANES), seq), pl.BlockSpec((1, rows, LANES), seq),
                  pl.BlockSpec((1, pcols, HEAD_DIM), seq), pl.BlockSpec((1, pcols, HEAD_DIM), seq)]
                 + k_specs + v_specs,
        out_specs=pl.BlockSpec((1, rows, ATT_Q_W), seq),
        scratch_shapes=[pltpu.VMEM((ATT_HEADS * rows, HEAD_DIM), F32),
                        pltpu.VMEM((ATT_HEADS * rows, 1), F32),
                        pltpu.VMEM((ATT_HEADS * rows, 1), F32),
                        pltpu.VMEM((ATT_HEADS * rows, HEAD_DIM), F32)],
    )
    return pl.pallas_call(
        kern,
        grid_spec=grid_spec,
        out_shape=jax.ShapeDtypeStruct((batch, rows, ATT_Q_W), BF16),
        compiler_params=_cparams("arbitrary", "arbitrary"),
        name="sample_attention",
    )(page_table, q, sc, thr, scn, kb_new, vb_new, *([cache_k] * pages), *([cache_v] * pages))


def _ssm_kernel(u_ref, s0r_ref, s0i_ref, ar_ref, ai_ref, bre_ref, bim_ref, cre_ref, cim_ref, d_ref,
                wglu_ref, z_ref, sr_out, si_out, sre_ref, sim_ref, str_ref, sti_ref, *, tc, nb, lc, cb):
    @pl.when(pl.program_id(0) == 0)
    def _():
        str_ref[...] = s0r_ref[...]
        sti_ref[...] = s0i_ref[...]

    width = u_ref.shape[-1]
    nst = sre_ref.shape[-1]
    u = u_ref[...].reshape(tc * nb, width)
    ub = u.astype(BF16)
    sb = cb * (nst // width)
    blocks = [(slice(k * cb, (k + 1) * cb), slice(k * sb, (k + 1) * sb)) for k in range(width // cb)]
    for csl, ssl in blocks:
        sre_ref[:, ssl] = jnp.dot(ub[:, csl], bre_ref[csl, ssl], preferred_element_type=F32)
        sim_ref[:, ssl] = jnp.dot(ub[:, csl], bim_ref[csl, ssl], preferred_element_type=F32)

    for ci in range(nst // lc):
        lsl = slice(ci * lc, (ci + 1) * lc)
        ar = jnp.broadcast_to(ar_ref[:, lsl], (nb, lc))
        ai = jnp.broadcast_to(ai_ref[:, lsl], (nb, lc))

        def step(j, carry):
            sr, si = carry
            r0 = pl.multiple_of(j * nb, nb)
            nr = ar * sr - ai * si + sre_ref[pl.ds(r0, nb), lsl]
            ni = ar * si + ai * sr + sim_ref[pl.ds(r0, nb), lsl]
            sre_ref[pl.ds(r0, nb), lsl] = nr
            sim_ref[pl.ds(r0, nb), lsl] = ni
            return nr, ni
        sr, si = lax.fori_loop(0, tc, step, (str_ref[:, lsl], sti_ref[:, lsl]))
        str_ref[:, lsl] = sr
        sti_ref[:, lsl] = si

    y = jnp.concatenate(
        [jnp.dot(sre_ref[:, ssl].astype(BF16), cre_ref[ssl, csl], preferred_element_type=F32)
         - jnp.dot(sim_ref[:, ssl].astype(BF16), cim_ref[ssl, csl], preferred_element_type=F32)
         for csl, ssl in blocks], axis=1) + u * d_ref[...]
    z = _gelu_tanh(y)
    z = z * _sigmoid(jnp.dot(z.astype(BF16), wglu_ref[...], preferred_element_type=F32))
    z_ref[...] = z.reshape(tc, nb, width).astype(BF16)
    sr_out[...] = str_ref[...]
    si_out[...] = sti_ref[...]


def _ssm_branch(u_tm, s0r, s0i, ar, ai, bre, bim, cre, cim, d, wglu, tc):
    t, nb, width = u_tm.shape
    nst = ar.shape[1]
    lc = min(256, nst)
    mxu = 256
    cb = mxu if width % mxu == 0 and (mxu % SSM_GROUP == 0) else width
    kern = functools.partial(_ssm_kernel, tc=tc, nb=nb, lc=lc, cb=cb)
    return pl.pallas_call(
        kern,
        grid=(t // tc,),
        in_specs=[pl.BlockSpec((tc, nb, width), lambda c: (c, 0, 0)),
                  _const_spec((nb, nst)), _const_spec((nb, nst)),
                  _const_spec((1, nst)), _const_spec((1, nst)),
                  _const_spec((width, nst)), _const_spec((width, nst)),
                  _const_spec((nst, width)), _const_spec((nst, width)),
                  _const_spec((1, width)), _const_spec((width, width))],
        out_specs=[pl.BlockSpec((tc, nb, width), lambda c: (c, 0, 0)),
                   pl.BlockSpec((nb, nst), lambda c: (0, 0)), pl.BlockSpec((nb, nst), lambda c: (0, 0))],
        out_shape=[jax.ShapeDtypeStruct((t, nb, width), BF16),
                   jax.ShapeDtypeStruct((nb, nst), F32), jax.ShapeDtypeStruct((nb, nst), F32)],
        scratch_shapes=[pltpu.VMEM((tc * nb, nst), F32), pltpu.VMEM((tc * nb, nst), F32),
                        pltpu.VMEM((nb, nst), F32), pltpu.VMEM((nb, nst), F32)],
        compiler_params=_cparams("arbitrary"),
        name="ssm_branch",
    )(u_tm, s0r, s0i, ar, ai, bre, bim, cre, cim, d, wglu)


MERGE_CW = 512


def _merge_kernel(o_ref, z_ref, x_ref, g_ref, wg_ref, wa_ref, ws_ref, wo_ref, out_ref, mg_ref):
    x = x_ref[...]
    d = x.shape[1]
    h = _rms(x, g_ref[...]).astype(BF16)
    o = o_ref[...]
    z = z_ref[...]
    for c in range(d // MERGE_CW):
        csl = slice(c * MERGE_CW, (c + 1) * MERGE_CW)
        ga = jnp.dot(h, wg_ref[:, csl], preferred_element_type=F32)
        gs = jnp.dot(h, wg_ref[:, d + c * MERGE_CW:d + (c + 1) * MERGE_CW], preferred_element_type=F32)
        a = jnp.dot(o, wa_ref[:, csl], preferred_element_type=F32)
        s = jnp.dot(z, ws_ref[:, csl], preferred_element_type=F32)
        mg_ref[:, csl] = (_sigmoid(ga) * a + _sigmoid(gs) * s).astype(BF16)
    out_ref[...] = x + jnp.dot(mg_ref[...], wo_ref[...], preferred_element_type=F32)


def _merge(o_att, z, z_map, x, g, wg, wa, ws, wo, tm):
    n, d = x.shape
    zw = ws.shape[0]
    row = lambda i: (i, 0)
    return pl.pallas_call(
        _merge_kernel,
        grid=(n // tm,),
        in_specs=[pl.BlockSpec((tm, ATT_Q_W), row), pl.BlockSpec((tm, zw), z_map),
                  pl.BlockSpec((tm, d), row), _const_spec((1, d)), _const_spec(wg.shape),
                  _const_spec(wa.shape), _const_spec(ws.shape), _const_spec(wo.shape)],
        out_specs=pl.BlockSpec((tm, d), row),
        out_shape=jax.ShapeDtypeStruct((n, d), F32),
        scratch_shapes=[pltpu.VMEM((tm, d), BF16)],
        compiler_params=_cparams("arbitrary"),
        name="gated_merge",
    )(o_att, z, x, g, wg, wa, ws, wo)


def _shift_rows(x, head):
    return jnp.concatenate([head, x[:x.shape[0] - head.shape[0], :]], axis=0)


def _ffn_kernel(x_ref, g_ref, init_ref, wup_ref, cw_ref, cb_ref, wdn_ref, out_ref, st_ref,
                h_ref, acc_ref, hist_ref, carry_ref, *, tm, tf, shift, tiles_per_seq):
    i = pl.program_id(0)
    f = pl.program_id(1)

    @pl.when(f == 0)
    def _():
        h_ref[...] = _rms(x_ref[...], g_ref[...]).astype(BF16)
        acc_ref[...] = jnp.zeros(acc_ref.shape, F32)

    first = (i % tiles_per_seq) == 0

    @pl.when(first)
    def _():
        hist_ref[...] = init_ref[...]

    @pl.when(jnp.logical_not(first))
    def _():
        hist_ref[...] = carry_ref[f]

    def conv_piece(csl):
        up = jnp.dot(h_ref[...], wup_ref[0, :, csl], preferred_element_type=F32)
        hist = hist_ref[:, csl]
        prev1 = _shift_rows(up, hist[shift:, :])
        prev2 = _shift_rows(up, hist)
        conv = (cw_ref[0, 2:3, csl] * up + cw_ref[0, 1:2, csl] * prev1 + cw_ref[0, 0:1, csl] * prev2
                + cb_ref[0, :, csl])
        return conv, up[tm - 2 * shift:, :]

    down = None
    for j in range(tf // FFN_SUB):
        gsl = slice(j * FFN_SUB, (j + 1) * FFN_SUB)
        vsl = slice(tf + j * FFN_SUB, tf + (j + 1) * FFN_SUB)
        gate, gate_tail = conv_piece(gsl)
        val, val_tail = conv_piece(vsl)
        for sl, tail in ((gsl, gate_tail), (vsl, val_tail)):
            carry_ref[f, :, sl] = tail
            st_ref[0, :, sl] = tail
        act = (_gelu_tanh(gate) * val).astype(BF16)
        part = jnp.dot(act, wdn_ref[0, j * FFN_SUB:(j + 1) * FFN_SUB, :], preferred_element_type=F32)
        down = part if down is None else down + part
    acc_ref[...] += down

    @pl.when(f == pl.num_programs(1) - 1)
    def _():
        out_ref[...] = x_ref[...] + acc_ref[...]


def _conv_ffn(x, g, init, wup, cw, cb, wdn, tm, shift, tiles_per_seq):
    n, d = x.shape
    nf, _, tf2 = wup.shape
    tf = tf2 // 2
    hist = 2 * shift
    kern = functools.partial(_ffn_kernel, tm=tm, tf=tf, shift=shift, tiles_per_seq=tiles_per_seq)
    return pl.pallas_call(
        kern,
        grid=(n // tm, nf),
        in_specs=[pl.BlockSpec((tm, d), lambda i, f: (i, 0)), _const_spec((1, d)),
                  pl.BlockSpec((hist, tf2), lambda i, f: (0, f)),
                  pl.BlockSpec((1, d, tf2), lambda i, f: (f, 0, 0)),
                  pl.BlockSpec((1, CONV_W, tf2), lambda i, f: (f, 0, 0)),
                  pl.BlockSpec((1, 1, tf2), lambda i, f: (f, 0, 0)),
                  pl.BlockSpec((1, tf, d), lambda i, f: (f, 0, 0))],
        out_specs=[pl.BlockSpec((tm, d), lambda i, f: (i, 0)),
                   pl.BlockSpec((1, hist, tf2), lambda i, f: (i, 0, f))],
        out_shape=[jax.ShapeDtypeStruct((n, d), F32),
                   jax.ShapeDtypeStruct((n // tm, hist, nf * tf2), F32)],
        scratch_shapes=[pltpu.VMEM((tm, d), BF16), pltpu.VMEM((tm, d), F32),
                        pltpu.VMEM((hist, tf2), F32), pltpu.VMEM((nf, hist, tf2), F32)],
        compiler_params=_cparams("arbitrary", "arbitrary"),
        name="conv_ffn",
    )(x, g, init, wup, cw, cb, wdn)


def _ple_kernel(x_ref, p_ref, gpg_ref, wpg_ref, wple_ref, gple_ref, out_ref):
    x = x_ref[...]
    ple = _rms(jnp.dot(p_ref[...].astype(BF16), wple_ref[...], preferred_element_type=F32), gple_ref[...])
    gate = _sigmoid(jnp.dot(_rms(x, gpg_ref[...]).astype(BF16), wpg_ref[...], preferred_element_type=F32))
    out_ref[...] = x + gate * ple


def _ple(x, p, gpg, wpg, wple, gple, tm):
    n, d = x.shape
    pd = p.shape[1]
    row = lambda i: (i, 0)
    return pl.pallas_call(
        _ple_kernel,
        grid=(n // tm,),
        in_specs=[pl.BlockSpec((tm, d), row), pl.BlockSpec((tm, pd), row), _const_spec((1, d)),
                  _const_spec((d, d)), _const_spec((pd, d)), _const_spec((1, d))],
        out_specs=pl.BlockSpec((tm, d), row),
        out_shape=jax.ShapeDtypeStruct((n, d), F32),
        compiler_params=_cparams("arbitrary"),
        name="ple_gate",
    )(x, p, gpg, wpg, wple, gple)


def _rope_tables(pos, head_dim, rot_dim):
    half = rot_dim // 2
    inv_freq = jnp.asarray(
        np.float32(ROPE_THETA) ** (-np.arange(half, dtype=np.float32) / np.float32(half)))
    ang = pos.astype(F32)[:, None] * inv_freq[None, :]
    cos, sin = jnp.cos(ang), jnp.sin(ang)
    n = pos.shape[0]
    ones = jnp.ones((n, head_dim - rot_dim), F32)
    zeros = jnp.zeros((n, head_dim - rot_dim), F32)
    zh = jnp.zeros((n, half), F32)
    c = jnp.concatenate([cos, cos, ones], axis=1)
    sa = jnp.concatenate([zh, sin, zeros], axis=1)
    sb = jnp.concatenate([-sin, zh, zeros], axis=1)
    reps = LANES // head_dim
    return jnp.stack([jnp.tile(t, (1, reps)) for t in (c, sa, sb)], axis=0)


def _ssm_params(lw):
    a_re = lw['ssm_a_re'].astype(F32)
    a_im = lw['ssm_a_im'].astype(F32)
    groups, nstate = a_re.shape
    dt = jnp.exp(lw['ssm_log_dt'].astype(F32))[:, None]
    mag = jnp.exp(a_re * dt)
    ang = a_im * dt
    ab_re, ab_im = mag * jnp.cos(ang), mag * jnp.sin(ang)
    den = a_re * a_re + a_im * a_im
    f_re = ((ab_re - 1.0) * a_re + ab_im * a_im) / den
    f_im = (ab_im * a_re - (ab_re - 1.0) * a_im) / den
    b_re = lw['ssm_b_re'].astype(F32)
    b_im = lw['ssm_b_im'].astype(F32)
    bb_re = f_re[..., None] * b_re - f_im[..., None] * b_im
    bb_im = f_re[..., None] * b_im + f_im[..., None] * b_re
    eye = jnp.eye(groups, dtype=F32)

    def in_map(bb):
        m = jnp.einsum('gnc,gh->gchn', bb, eye)
        return m.reshape(groups * SSM_GROUP, groups * nstate).astype(BF16)

    def out_map(cc):
        m = jnp.einsum('gcn,gh->gnhc', cc.astype(F32), eye)
        return m.reshape(groups * nstate, groups * SSM_GROUP).astype(BF16)

    return dict(ar=ab_re.reshape(1, -1), ai=ab_im.reshape(1, -1),
                bre=in_map(bb_re), bim=in_map(bb_im),
                cre=out_map(lw['ssm_c_re']), cim=out_map(lw['ssm_c_im']),
                d=lw['ssm_d'].astype(F32).reshape(1, -1), wglu=lw['w_glu'].astype(BF16))


def _chunk_features(a, d_ff, tf):
    nf = -(-d_ff // tf)
    pad = [(0, 0)] * (a.ndim - 1) + [(0, nf * tf - d_ff)]
    gate = jnp.pad(a[..., :d_ff], pad).reshape(a.shape[:-1] + (nf, tf))
    val = jnp.pad(a[..., d_ff:], pad).reshape(a.shape[:-1] + (nf, tf))
    return jnp.concatenate([gate, val], axis=-1)


def _unchunk_features(a, d_ff, tf):
    nf = a.shape[-1] // (2 * tf)
    a = a.reshape(a.shape[:-1] + (nf, 2, tf))
    gate = a[..., 0, :].reshape(a.shape[:-3] + (nf * tf,))[..., :d_ff]
    val = a[..., 1, :].reshape(a.shape[:-3] + (nf * tf,))[..., :d_ff]
    return jnp.concatenate([gate, val], axis=-1)


FFN_TF = 512
FFN_SUB = 256


def _layer_weights(lw):
    w_in = lw['w_in']
    d = w_in.shape[0]
    o = np.cumsum([0, ATT_Q_W, ATT_KV_W, ATT_KV_W, IDX_Q_W, IDX_DIM, IDX_HEADS])
    ssm_w = lw['ssm_d'].shape[0]
    w_q_kv = w_in[:, :o[3]]
    w_iq, w_ik, w_iw = w_in[:, o[3]:o[4]], w_in[:, o[4]:o[5]], w_in[:, o[5]:o[6]]
    w_u = w_in[:, o[6]:o[6] + ssm_w]
    w_gates = w_in[:, o[6] + ssm_w:]
    w_idx = jnp.concatenate([w_iq, w_ik, w_ik, w_iw, jnp.zeros((d, LANES - IDX_HEADS), w_in.dtype)], axis=1)
    d_ff = lw['w_down'].shape[0]
    nf = -(-d_ff // FFN_TF)
    wup = jnp.transpose(_chunk_features(lw['w_up'], d_ff, FFN_TF), (1, 0, 2)).astype(BF16)
    wdn = jnp.pad(lw['w_down'], ((0, nf * FFN_TF - d_ff), (0, 0))).reshape(nf, FFN_TF, d).astype(BF16)
    cw = jnp.transpose(_chunk_features(lw['conv_w'], d_ff, FFN_TF), (1, 0, 2)).astype(F32)
    cb = _chunk_features(lw['conv_b'], d_ff, FFN_TF).reshape(nf, 1, 2 * FFN_TF).astype(F32)
    out = dict(
        g_mix=lw['g_mix'].reshape(1, d), g_q=lw['g_q'].reshape(1, HEAD_DIM), g_k=lw['g_k'].reshape(1, HEAD_DIM),
        w_qkv=w_q_kv.astype(BF16), w_idx=w_idx.astype(BF16), w_u=w_u.astype(BF16),
        w_gates=w_gates.astype(BF16),
        w_att_br=lw['w_att_br'].astype(BF16), w_ssm_br=lw['w_ssm_br'].astype(BF16), w_o=lw['w_o'].astype(BF16),
        g_ffn=lw['g_ffn'].reshape(1, d), wup=wup, wdn=wdn, cw=cw, cb=cb, d_ff=d_ff,
        g_pg=lw['g_pg'].reshape(1, d), w_pg=lw['w_pg'].astype(BF16), w_ple=lw['w_ple'].astype(BF16),
        g_ple=lw['g_ple'].reshape(1, d))
    out.update(_ssm_params(lw))
    return out


def _row_tile(n, cap):
    return cap if n % cap == 0 else n


def _prompt_layer(x3, p3, w):
    batch, seq, d = x3.shape
    n = batch * seq
    x = x3.reshape(n, d)
    tm = _row_tile(seq, 512)
    tpb = seq // tm
    pos = jnp.arange(seq)
    tab_att = _rope_tables(pos, HEAD_DIM, ROT_DIM)
    tab_idx = _rope_tables(pos, IDX_DIM, IDX_ROT_DIM)

    q, k, v, kb, vb = _qkv_proj(x, w['g_mix'], w['w_qkv'], w['g_q'], w['g_k'], tab_att, tm)
    iq, ik2, ikb, iw = _idx_proj(x, w['g_mix'], w['w_idx'], tab_idx, tm)
    ssm_w = w['w_u'].shape[1]
    u_tm = _norm_matmul(x, w['g_mix'], w['w_u'], tm, ssm_w,
                        jax.ShapeDtypeStruct((seq, batch * ssm_w), F32),
                        lambda i, j: (i % tpb, i // tpb), "u_proj").reshape(seq, batch, ssm_w)

    o_att = _prompt_attention(q, iq, iw, ikb, kb, vb, batch, seq)

    nst = w['ar'].shape[1]
    zeros = jnp.zeros((batch, nst), F32)
    z_tm, s_re, s_im = _ssm_branch(u_tm, zeros, zeros, w['ar'], w['ai'], w['bre'], w['bim'],
                                   w['cre'], w['cim'], w['d'], w['wglu'], tc=min(32, seq))
    tmm = _row_tile(seq, 256)
    tpbm = seq // tmm
    x1 = _merge(o_att, z_tm.reshape(seq, batch * ssm_w), lambda i: (i % tpbm, i // tpbm),
                x, w['g_mix'], w['w_gates'], w['w_att_br'], w['w_ssm_br'], w['w_o'], tmm)

    nf, _, tf2 = w['wup'].shape
    x2, conv = _conv_ffn(x1, w['g_ffn'], jnp.zeros((CONV_W - 1, nf * tf2), F32),
                         w['wup'], w['cw'], w['cb'], w['wdn'], tm, 1, tpb)
    conv = _unchunk_features(conv[tpb - 1::tpb], w['d_ff'], tf2 // 2)
    y = _ple(x2, p3.reshape(n, -1), w['g_pg'], w['w_pg'], w['w_ple'], w['g_ple'], tm)

    groups = nst // SSM_STATE
    return (y.reshape(batch, seq, d), k.reshape(batch, seq, KV_HEADS, HEAD_DIM),
            v.reshape(batch, seq, KV_HEADS, HEAD_DIM), ik2[:, :IDX_DIM].reshape(batch, seq, IDX_DIM),
            s_re.reshape(batch, groups, SSM_STATE), s_im.reshape(batch, groups, SSM_STATE), conv)


def _pad_rows(a, rows):
    pad = [(0, 0)] * a.ndim
    pad[1] = (0, rows - a.shape[1])
    return jnp.pad(a, pad)


def _sample_layer(x3, p3, cache_k, cache_v, cache_ik, s0_re, s0_im, conv0, page_table, w):
    batch, t_new, d = x3.shape
    n = batch * t_new
    n_pages = page_table.shape[1]
    past = n_pages * PAGE_SIZE
    x = x3.reshape(n, d)
    pos = past + (jnp.arange(n) % t_new)
    tab_att = _rope_tables(pos, HEAD_DIM, ROT_DIM)
    tab_idx = _rope_tables(pos, IDX_DIM, IDX_ROT_DIM)

    q, k, v, kb, vb = _qkv_proj(x, w['g_mix'], w['w_qkv'], w['g_q'], w['g_k'], tab_att, n)
    iq, ik2, ikb, iw = _idx_proj(x, w['g_mix'], w['w_idx'], tab_idx, n)
    ssm_w = w['w_u'].shape[1]
    u = _norm_matmul(x, w['g_mix'], w['w_u'], n, ssm_w,
                     jax.ShapeDtypeStruct((n, ssm_w), F32), lambda i, j: (i, j), "u_proj_s")

    per_seq = lambda a: a.reshape(batch, t_new, a.shape[-1])
    pages = max(p for p in (32, 16, 8, 4, 2, 1) if n_pages % p == 0)
    sc, thr, scn = _sample_scores(
        page_table, _pad_rows(per_seq(iq), SUBLANES), _pad_rows(per_seq(iw), SUBLANES),
        jnp.swapaxes(_pad_rows(per_seq(ikb[:, :IDX_DIM]), LANES), 1, 2), cache_ik, pages, t_new)
    new_page = lambda a: _pad_rows(a.reshape(batch, t_new * KV_HEADS, HEAD_DIM), PAGE_SIZE * KV_HEADS)
    o_att = _sample_attention(
        page_table, _pad_rows(per_seq(q), SUBLANES), sc, thr, scn,
        new_page(k), new_page(v), cache_k, cache_v, pages)
    o_att = o_att[:, :t_new].reshape(n, ATT_Q_W)

    nst = w['ar'].shape[1]
    u_tm = jnp.transpose(u.reshape(batch, t_new, ssm_w), (1, 0, 2))
    z_tm, s_re, s_im = _ssm_branch(u_tm, s0_re.reshape(batch, nst).astype(F32),
                                   s0_im.reshape(batch, nst).astype(F32), w['ar'], w['ai'], w['bre'],
                                   w['bim'], w['cre'], w['cim'], w['d'], w['wglu'], tc=t_new)
    z = jnp.transpose(z_tm, (1, 0, 2)).reshape(n, ssm_w)
    x1 = _merge(o_att, z, lambda i: (i, 0), x, w['g_mix'], w['w_gates'], w['w_att_br'], w['w_ssm_br'],
                w['w_o'], n)

    nf, _, tf2 = w['wup'].shape
    d_ff = w['d_ff']
    x1_tm = jnp.transpose(x1.reshape(batch, t_new, d), (1, 0, 2)).reshape(n, d)
    init = _chunk_features(jnp.transpose(conv0.astype(F32), (1, 0, 2)), d_ff, tf2 // 2)
    init = init.reshape((CONV_W - 1) * batch, nf * tf2)
    x2_tm, conv = _conv_ffn(x1_tm, w['g_ffn'], init, w['wup'], w['cw'], w['cb'], w['wdn'], n, batch, 1)
    conv = _unchunk_features(conv.reshape(CONV_W - 1, batch, nf * tf2), d_ff, tf2 // 2)
    conv = jnp.transpose(conv, (1, 0, 2))
    x2 = jnp.transpose(x2_tm.reshape(t_new, batch, d), (1, 0, 2)).reshape(n, d)
    y = _ple(x2, p3.reshape(n, -1), w['g_pg'], w['w_pg'], w['w_ple'], w['g_ple'], n)

    groups = nst // SSM_STATE
    return (y.reshape(batch, t_new, d), k.reshape(batch, t_new, KV_HEADS, HEAD_DIM),
            v.reshape(batch, t_new, KV_HEADS, HEAD_DIM), ik2[:, :IDX_DIM].reshape(batch, t_new, IDX_DIM),
            s_re.reshape(batch, groups, SSM_STATE), s_im.reshape(batch, groups, SSM_STATE), conv)


def kernel(x_prompt, x_sample, cache_k, cache_v, cache_idx_k, state_ssm_re, state_ssm_im, state_conv, page_table, p_prompt, p_sample, g_mix, w_in, g_q, g_k, ssm_a_re, ssm_a_im, ssm_log_dt, ssm_b_re, ssm_b_im, ssm_c_re, ssm_c_im, ssm_d, w_glu, w_att_br, w_ssm_br, w_o, g_ffn, w_up, conv_w, conv_b, w_down, w_ple, g_ple, g_pg, w_pg):
    depth = w_in.shape[0]
    n_pool = cache_k.shape[1]
    pool_k = cache_k.reshape(depth * n_pool, PAGE_SIZE * KV_HEADS, HEAD_DIM)
    pool_v = cache_v.reshape(depth * n_pool, PAGE_SIZE * KV_HEADS, HEAD_DIM)
    pool_ik = jnp.swapaxes(cache_idx_k, 2, 3).reshape(depth * n_pool, IDX_DIM, PAGE_SIZE)
    xp, xs = x_prompt, x_sample
    outs_p, outs_s = [], []
    for i in range(depth):
        lw = dict(g_mix=g_mix[i], w_in=w_in[i], g_q=g_q[i], g_k=g_k[i],
                  ssm_a_re=ssm_a_re[i], ssm_a_im=ssm_a_im[i], ssm_log_dt=ssm_log_dt[i],
                  ssm_b_re=ssm_b_re[i], ssm_b_im=ssm_b_im[i], ssm_c_re=ssm_c_re[i], ssm_c_im=ssm_c_im[i],
                  ssm_d=ssm_d[i], w_glu=w_glu[i], w_att_br=w_att_br[i], w_ssm_br=w_ssm_br[i], w_o=w_o[i],
                  g_ffn=g_ffn[i], w_up=w_up[i], conv_w=conv_w[i], conv_b=conv_b[i], w_down=w_down[i],
                  w_ple=w_ple[i], g_ple=g_ple[i], g_pg=g_pg[i], w_pg=w_pg[i])
        w = _layer_weights(lw)
        xp, *rest_p = _prompt_layer(xp, p_prompt[i], w)
        outs_p.append(rest_p)
        xs, *rest_s = _sample_layer(xs, p_sample[i], pool_k, pool_v, pool_ik, state_ssm_re[i],
                                    state_ssm_im[i], state_conv[i], page_table + i * n_pool, w)
        outs_s.append(rest_s)
    stack = lambda outs, j: jnp.stack([o[j] for o in outs], 0)
    return ((xp, xs) + tuple(stack(outs_p, j) for j in range(6))
            + tuple(stack(outs_s, j) for j in range(6)))
```

```python
import functools

import numpy as np
import jax
import jax.numpy as jnp
from jax import lax
from jax.experimental import pallas as pl
from jax.experimental.pallas import tpu as pltpu

F32 = jnp.float32
BF16 = jnp.bfloat16
I32 = jnp.int32

ATT_HEADS = 8
KV_HEADS = 2
HEAD_DIM = 128
GROUP = ATT_HEADS // KV_HEADS
ROT_DIM = HEAD_DIM // 4
ROPE_THETA = 500000.0
IDX_HEADS = 16
IDX_DIM = 64
IDX_ROT_DIM = IDX_DIM // 4
IDX_SCALE = (IDX_HEADS ** -0.5) * (IDX_DIM ** -0.5)
TOPK_MAX = 256
Q_BLOCK = 128
PAGE_SIZE = 128
SSM_GROUP = 16
SSM_STATE = 64
CONV_W = 3
EPS = 1e-6

LANES = 128
SUBLANES = 8
NEG = -1e30
INT_MIN = -2 ** 31
VMEM_LIMIT = 56 * 1024 * 1024

ATT_Q_W = ATT_HEADS * HEAD_DIM
ATT_KV_W = KV_HEADS * HEAD_DIM
IDX_Q_W = IDX_HEADS * IDX_DIM


def _cparams(*sem):
    return pltpu.CompilerParams(dimension_semantics=sem, vmem_limit_bytes=VMEM_LIMIT)


def _const_spec(shape):
    nd = len(shape)
    return pl.BlockSpec(shape, lambda *_: (0,) * nd, pipeline_mode=pl.Buffered(1))


def _rms(x, g):
    ms = jnp.mean(x * x, axis=-1, keepdims=True)
    return (x * lax.rsqrt(ms + EPS)) * g


def _gelu_tanh(x):
    return 0.5 * x * (1.0 + jnp.tanh(0.7978845608028654 * (x + 0.044715 * (x * x * x))))


def _sigmoid(x):
    return 1.0 / (1.0 + jnp.exp(-x))


def _rope(x, tab_ref, half):
    return (x * tab_ref[0] + pltpu.roll(x, half, 1) * tab_ref[1]
            + pltpu.roll(x, LANES - half, 1) * tab_ref[2])


def _dot_t(a, b):
    return lax.dot_general(a, b, (((1,), (1,)), ((), ())), preferred_element_type=F32)


def _in_proj_kernel(x_ref, g_ref, wq_ref, wi_ref, wu_ref, gq_ref, gk_ref, ta_ref, ti_ref,
                    q_ref, k_ref, v_ref, kb_ref, vb_ref, iq_ref, ik_ref, ikb_ref, iw_ref, u_ref):
    tm = x_ref.shape[0]
    h = _rms(x_ref[...], g_ref[...]).astype(BF16)

    y = jnp.dot(h, wq_ref[...], preferred_element_type=F32)
    half = ROT_DIM // 2
    for hh in range(ATT_HEADS):
        sl = slice(hh * HEAD_DIM, (hh + 1) * HEAD_DIM)
        qh = _rope(_rms(y[:, sl], gq_ref[...]), ta_ref, half)
        q_ref[:, sl] = (qh * (HEAD_DIM ** -0.5)).astype(BF16)
    for hh in range(KV_HEADS):
        sl = slice(hh * HEAD_DIM, (hh + 1) * HEAD_DIM)
        kh = _rope(_rms(y[:, ATT_Q_W + hh * HEAD_DIM:ATT_Q_W + (hh + 1) * HEAD_DIM], gk_ref[...]),
                   ta_ref, half)
        vh = y[:, ATT_Q_W + ATT_KV_W + hh * HEAD_DIM:ATT_Q_W + ATT_KV_W + (hh + 1) * HEAD_DIM]
        k_ref[pl.ds(hh, tm, stride=KV_HEADS), :] = kh
        v_ref[pl.ds(hh, tm, stride=KV_HEADS), :] = vh
        kb_ref[:, sl] = kh.astype(BF16)
        vb_ref[:, sl] = vh.astype(BF16)

    y = jnp.dot(h, wi_ref[...], preferred_element_type=F32)
    half = IDX_ROT_DIM // 2
    for s in range(IDX_Q_W // LANES):
        sl = slice(s * LANES, (s + 1) * LANES)
        iq_ref[:, sl] = _rope(y[:, sl], ti_ref, half).astype(BF16)
    ik2 = _rope(y[:, IDX_Q_W:IDX_Q_W + LANES], ti_ref, half)
    ik_ref[...] = ik2
    ikb_ref[...] = ik2.astype(BF16)
    iw_ref[...] = y[:, IDX_Q_W + LANES:] * IDX_SCALE

    u_ref[...] = jnp.dot(h, wu_ref[...], preferred_element_type=F32)


def _in_proj(x, g, wq, wi, wu, gq, gk, tab_att, tab_idx, tm, u_shape, u_map):
    n, d = x.shape
    uw = wu.shape[1]
    p_blocks = tab_att.shape[1] // tm
    row = lambda i: (i, 0)
    tab = lambda i: (0, i % p_blocks, 0)
    return pl.pallas_call(
        _in_proj_kernel,
        grid=(n // tm,),
        in_specs=[pl.BlockSpec((tm, d), row), _const_spec((1, d)), _const_spec(wq.shape), _const_spec(wi.shape),
                  _const_spec(wu.shape), _const_spec((1, HEAD_DIM)), _const_spec((1, HEAD_DIM)),
                  pl.BlockSpec((3, tm, LANES), tab), pl.BlockSpec((3, tm, LANES), tab)],
        out_specs=[pl.BlockSpec((tm, ATT_Q_W), row), pl.BlockSpec((KV_HEADS * tm, HEAD_DIM), row),
                   pl.BlockSpec((KV_HEADS * tm, HEAD_DIM), row), pl.BlockSpec((tm, ATT_KV_W), row),
                   pl.BlockSpec((tm, ATT_KV_W), row), pl.BlockSpec((tm, IDX_Q_W), row),
                   pl.BlockSpec((tm, LANES), row), pl.BlockSpec((tm, LANES), row), pl.BlockSpec((tm, LANES), row),
                   pl.BlockSpec((tm, uw), u_map)],
        out_shape=[jax.ShapeDtypeStruct((n, ATT_Q_W), BF16),
                   jax.ShapeDtypeStruct((KV_HEADS * n, HEAD_DIM), F32),
                   jax.ShapeDtypeStruct((KV_HEADS * n, HEAD_DIM), F32),
                   jax.ShapeDtypeStruct((n, ATT_KV_W), BF16), jax.ShapeDtypeStruct((n, ATT_KV_W), BF16),
                   jax.ShapeDtypeStruct((n, IDX_Q_W), BF16), jax.ShapeDtypeStruct((n, LANES), F32),
                   jax.ShapeDtypeStruct((n, LANES), BF16), jax.ShapeDtypeStruct((n, LANES), F32),
                   jax.ShapeDtypeStruct(u_shape, F32)],
        compiler_params=_cparams("arbitrary"),
        name="in_proj",
    )(x, g, wq, wi, wu, gq, gk, tab_att, tab_idx)


def _sort_key(sc):
    bits = pltpu.bitcast(sc, I32)
    return bits ^ ((bits >> 31) & jnp.int32(0x7FFFFFFF))


def _kth_largest(count_ge, shape, topk):
    def body(it, t):
        cand = t + lax.shift_left(jnp.int32(1), 31 - it)
        return jnp.where(count_ge(cand) >= topk, cand, t)
    t = lax.fori_loop(0, 32, body, jnp.full(shape, INT_MIN, I32))
    return jnp.maximum(t, INT_MIN + 1)


def _tree(op, parts):
    while len(parts) > 1:
        nxt = [op(parts[j], parts[j + 1]) for j in range(0, len(parts) - 1, 2)]
        if len(parts) % 2:
            nxt.append(parts[-1])
        parts = nxt
    return parts[0]


def _row_groups(op, x):
    return _tree(op, [x[r:r + SUBLANES, :] for r in range(0, x.shape[0], SUBLANES)])


def _indexer_operands(iq, iw, lhs_ref, wb_ref, rows):
    for h in range(IDX_HEADS):
        lhs_ref[h * rows:(h + 1) * rows, :] = iq[:, h * IDX_DIM:(h + 1) * IDX_DIM].astype(F32)
        wb_ref[h * rows:(h + 1) * rows, :] = jnp.broadcast_to(iw[:, h:h + 1], (rows, LANES))


def _indexer_scores(lhs_ref, wb_ref, keys_t, rows):
    n_keys = keys_t.shape[1]
    r = jnp.dot(lhs_ref[...].astype(BF16), keys_t, preferred_element_type=F32)
    sc = jnp.zeros((rows, n_keys), F32)
    for h in range(IDX_HEADS):
        wh = jnp.concatenate([wb_ref[h * rows:(h + 1) * rows, :]] * (n_keys // LANES), axis=1)
        sc = sc + wh * jnp.maximum(r[h * rows:(h + 1) * rows, :], 0.0)
    return sc


def _prompt_attn_kernel(q_ref, iq_ref, iw_ref, ikb_ref, kb_ref, vb_ref, o_ref,
                        lhs_ref, q2_ref, vt_ref, key_ref, s_ref, m_ref, l_ref, acc_ref, *, seq, kc, topk):
    qb = Q_BLOCK
    npair = ATT_HEADS // 2
    i = pl.program_id(1)
    t0 = i * qb
    nch = (t0 + qb + kc - 1) // kc

    @pl.when(i == 0)
    def _():
        def transpose_chunk(c, carry):
            c0 = pl.multiple_of(c * kc, kc)
            vt_ref[c] = vb_ref[pl.ds(c0, kc), :].astype(F32).T.astype(BF16)
            return carry
        lax.fori_loop(0, seq // kc, transpose_chunk, 0)

    lane = lax.broadcasted_iota(I32, (qb, LANES), 1)
    for s in range(IDX_Q_W // LANES):
        x = iq_ref[:, s * LANES:(s + 1) * LANES].astype(F32)
        lhs_ref[(2 * s) * qb:(2 * s + 1) * qb, :] = jnp.where(lane < IDX_DIM, x, 0.0).astype(BF16)
        lhs_ref[(2 * s + 1) * qb:(2 * s + 2) * qb, :] = jnp.where(lane >= IDX_DIM, x, 0.0).astype(BF16)
    iwt = iw_ref[...].T
    for p in range(npair):
        for r in range(2):
            hsl = slice((2 * p + r) * HEAD_DIM, (2 * p + r + 1) * HEAD_DIM)
            q2_ref[p, r * qb:(r + 1) * qb, :] = q_ref[:, hsl]

    key_pos = lax.broadcasted_iota(I32, (kc, LANES), 0)
    q_pos = t0 + lax.broadcasted_iota(I32, (kc, LANES), 1)

    def score_chunk(c, carry):
        c0 = pl.multiple_of(c * kc, kc)
        ikc = ikb_ref[pl.ds(c0, kc), :]
        sc = jnp.zeros((kc, LANES), F32)
        for s in range(IDX_HEADS // 2):
            r = _dot_t(ikc, lhs_ref[2 * s * qb:(2 * s + 2) * qb, :])
            sc = (sc + iwt[2 * s:2 * s + 1, :] * jnp.maximum(r[:, :LANES], 0.0)
                  + iwt[2 * s + 1:2 * s + 2, :] * jnp.maximum(r[:, LANES:], 0.0))
        key_ref[c] = jnp.where(c0 + key_pos <= q_pos, _sort_key(sc), INT_MIN)
        return carry
    lax.fori_loop(0, nch, score_chunk, 0)

    def count_ge(cand):
        def chunk(c, acc):
            return acc + _row_groups(jnp.add, jnp.where(key_ref[c] >= cand, 1.0, 0.0))
        acc = lax.fori_loop(0, nch, chunk, jnp.zeros((SUBLANES, LANES), F32))
        return jnp.sum(acc, axis=0, keepdims=True)
    thr = _kth_largest(count_ge, (1, LANES), float(topk))

    m_ref[...] = jnp.full(m_ref.shape, NEG, F32)
    l_ref[...] = jnp.zeros(l_ref.shape, F32)
    acc_ref[...] = jnp.zeros(acc_ref.shape, F32)

    def score_pass(c, carry):
        c0 = pl.multiple_of(c * kc, kc)
        sel = key_ref[c] >= thr
        for p in range(npair):
            gsl = slice((2 * p) // GROUP * HEAD_DIM, ((2 * p) // GROUP + 1) * HEAD_DIM)
            s = _dot_t(kb_ref[pl.ds(c0, kc), gsl], q2_ref[p])
            s = jnp.concatenate([jnp.where(sel, s[:, :LANES], NEG), jnp.where(sel, s[:, LANES:], NEG)], axis=1)
            s_ref[c, p] = s
            m_ref[p] = jnp.maximum(m_ref[p], _row_groups(jnp.maximum, s))
        return carry
    lax.fori_loop(0, nch, score_pass, 0)
    m = [jnp.max(m_ref[p], axis=0, keepdims=True) for p in range(npair)]

    def value_pass(c, carry):
        for p in range(npair):
            gsl = slice((2 * p) // GROUP * HEAD_DIM, ((2 * p) // GROUP + 1) * HEAD_DIM)
            pr = jnp.exp(s_ref[c, p] - m[p])
            l_ref[p] += _row_groups(jnp.add, pr)
            acc_ref[p] += jnp.dot(vt_ref[c, gsl, :], pr.astype(BF16),
                                  preferred_element_type=F32)
        return carry
    lax.fori_loop(0, nch, value_pass, 0)

    for p in range(npair):
        o_t = acc_ref[p] / jnp.sum(l_ref[p], axis=0, keepdims=True)
        for r in range(2):
            hsl = slice((2 * p + r) * HEAD_DIM, (2 * p + r + 1) * HEAD_DIM)
            o_ref[:, hsl] = o_t[:, r * qb:(r + 1) * qb].T.astype(BF16)


def _prompt_attention(q, iq, iw, ikb, kb, vb, batch, seq):
    kc = min(256, seq)
    topk = min(TOPK_MAX, seq // 4)
    nq = seq // Q_BLOCK
    blk = lambda b, i: (b * nq + i, 0)
    full = lambda b, i: (b, 0)
    kern = functools.partial(_prompt_attn_kernel, seq=seq, kc=kc, topk=topk)
    return pl.pallas_call(
        kern,
        grid=(batch, nq),
        in_specs=[pl.BlockSpec((Q_BLOCK, ATT_Q_W), blk), pl.BlockSpec((Q_BLOCK, IDX_Q_W), blk),
                  pl.BlockSpec((Q_BLOCK, LANES), blk), pl.BlockSpec((seq, LANES), full),
                  pl.BlockSpec((seq, ATT_KV_W), full), pl.BlockSpec((seq, ATT_KV_W), full)],
        out_specs=pl.BlockSpec((Q_BLOCK, ATT_Q_W), blk),
        out_shape=jax.ShapeDtypeStruct((batch * seq, ATT_Q_W), BF16),
        scratch_shapes=[pltpu.VMEM((IDX_HEADS * Q_BLOCK, LANES), BF16),
                        pltpu.VMEM((ATT_HEADS // 2, 2 * Q_BLOCK, HEAD_DIM), BF16),
                        pltpu.VMEM((seq // kc, ATT_KV_W, kc), BF16),
                        pltpu.VMEM((seq // kc, kc, Q_BLOCK), I32),
                        pltpu.VMEM((seq // kc, ATT_HEADS // 2, kc, 2 * Q_BLOCK), F32),
                        pltpu.VMEM((ATT_HEADS // 2, SUBLANES, 2 * Q_BLOCK), F32),
                        pltpu.VMEM((ATT_HEADS // 2, SUBLANES, 2 * Q_BLOCK), F32),
                        pltpu.VMEM((ATT_HEADS // 2, HEAD_DIM, 2 * Q_BLOCK), F32)],
        compiler_params=_cparams("arbitrary", "arbitrary"),
        name="prompt_attention",
    )(q, iq, iw, ikb, kb, vb)


def _sample_score_kernel(pt_ref, iq_ref, iw_ref, ikn_ref, *rest, pages, n_new):
    page_refs = rest[:pages]
    sc_ref, thr_ref, scn_ref, lhs_ref, wb_ref, key_ref = rest[pages:]
    del pt_ref
    c = pl.program_id(1)
    nchunks = pl.num_programs(1)
    rows = SUBLANES

    @pl.when(c == 0)
    def _():
        _indexer_operands(iq_ref[0], iw_ref[0], lhs_ref, wb_ref, rows)

    keys_t = jnp.concatenate([page_refs[p][0] for p in range(pages)], axis=1).astype(BF16)
    sc = _indexer_scores(lhs_ref, wb_ref, keys_t, rows)
    sc_ref[0] = sc
    key = _sort_key(sc)
    for p in range(pages):
        key_ref[c * pages + p] = key[:, p * PAGE_SIZE:(p + 1) * PAGE_SIZE]

    @pl.when(c == nchunks - 1)
    def _():
        scn = _indexer_scores(lhs_ref, wb_ref, ikn_ref[0], rows)
        scn_ref[0] = scn
        t = lax.broadcasted_iota(I32, (rows, LANES), 0)
        j = lax.broadcasted_iota(I32, (rows, LANES), 1)
        keyn = jnp.where(j <= t, _sort_key(scn), INT_MIN)
        total = key_ref.shape[0]

        def count_ge(cand):
            hit = _tree(jnp.add, [jnp.where(key_ref[pg] >= cand, 1.0, 0.0) for pg in range(total)])
            return jnp.sum(hit + jnp.where(keyn >= cand, 1.0, 0.0), axis=1, keepdims=True)
        topk = float(min(TOPK_MAX, (total * PAGE_SIZE + n_new) // 4))
        thr = _kth_largest(count_ge, (rows, 1), topk)
        thr_ref[0] = jnp.broadcast_to(thr, (rows, LANES))


def _sample_scores(page_table, iq, iw, ikb_new, cache_ik, pages, n_new):
    batch, n_pages = page_table.shape
    rows = SUBLANES
    past = n_pages * PAGE_SIZE
    nchunks = n_pages // pages
    seq = lambda b, c, pt: (b, 0, 0)
    page_specs = [pl.BlockSpec((1, IDX_DIM, PAGE_SIZE),
                               functools.partial(lambda b, c, pt, p: (pt[b, c * pages + p], 0, 0), p=p))
                  for p in range(pages)]
    kern = functools.partial(_sample_score_kernel, pages=pages, n_new=n_new)
    grid_spec = pltpu.PrefetchScalarGridSpec(
        num_scalar_prefetch=1,
        grid=(batch, nchunks),
        in_specs=[pl.BlockSpec((1, rows, IDX_Q_W), seq), pl.BlockSpec((1, rows, LANES), seq),
                  pl.BlockSpec((1, IDX_DIM, LANES), seq)] + page_specs,
        out_specs=[pl.BlockSpec((1, rows, pages * PAGE_SIZE), lambda b, c, pt: (b, 0, c)),
                   pl.BlockSpec((1, rows, LANES), seq), pl.BlockSpec((1, rows, LANES), seq)],
        scratch_shapes=[pltpu.VMEM((IDX_HEADS * rows, IDX_DIM), F32),
                        pltpu.VMEM((IDX_HEADS * rows, LANES), F32),
                        pltpu.VMEM((n_pages, rows, PAGE_SIZE), I32)],
    )
    return pl.pallas_call(
        kern,
        grid_spec=grid_spec,
        out_shape=[jax.ShapeDtypeStruct((batch, rows, past), F32),
                   jax.ShapeDtypeStruct((batch, rows, LANES), I32),
                   jax.ShapeDtypeStruct((batch, rows, LANES), F32)],
        compiler_params=_cparams("arbitrary", "arbitrary"),
        name="sample_scores",
    )(page_table, iq, iw, ikb_new, *([cache_ik] * pages))


def _sample_attn_kernel(pt_ref, q_ref, sc_ref, thr_ref, scn_ref, kn_ref, vn_ref, *rest, pages):
    k_refs = rest[:pages]
    v_refs = rest[pages:2 * pages]
    o_ref, qa_ref, m_ref, l_ref, acc_ref = rest[2 * pages:]
    del pt_ref
    c = pl.program_id(1)
    nchunks = pl.num_programs(1)
    rows = SUBLANES
    pcols = PAGE_SIZE * KV_HEADS
    thr = thr_ref[0][:, 0:1]

    @pl.when(c == 0)
    def _():
        for h in range(ATT_HEADS):
            qa_ref[h * rows:(h + 1) * rows, :] = q_ref[0, :, h * HEAD_DIM:(h + 1) * HEAD_DIM].astype(F32)
        m_ref[...] = jnp.full(m_ref.shape, NEG, F32)
        l_ref[...] = jnp.zeros(l_ref.shape, F32)
        acc_ref[...] = jnp.zeros(acc_ref.shape, F32)

    kk = lax.broadcasted_iota(I32, (PAGE_SIZE, KV_HEADS * pcols), 0)
    cc = lax.broadcasted_iota(I32, (PAGE_SIZE, KV_HEADS * pcols), 1)
    head_bits = KV_HEADS.bit_length() - 1
    col = cc & (pcols - 1)
    expand = jnp.where(((col >> head_bits) == kk) & ((col & (KV_HEADS - 1)) == (cc >> (pcols.bit_length() - 1))),
                       1.0, 0.0).astype(BF16)

    def update(sel, kchunk, vchunk):
        n = sel.shape[1] // PAGE_SIZE
        stack = jnp.concatenate([sel[:, p * PAGE_SIZE:(p + 1) * PAGE_SIZE] for p in range(n)], axis=0)
        e = jnp.dot(stack.astype(BF16), expand, preferred_element_type=F32)
        mask = jnp.concatenate(
            [jnp.concatenate([jnp.concatenate(
                [e[p * rows:(p + 1) * rows, g * pcols:(g + 1) * pcols] for p in range(n)], axis=1)] * GROUP, axis=0)
             for g in range(KV_HEADS)], axis=0)
        s = jnp.where(mask > 0.5, _dot_t(qa_ref[...].astype(BF16), kchunk), NEG)
        m_old = m_ref[...]
        m_new = jnp.maximum(m_old, jnp.max(s, axis=1, keepdims=True))
        alpha = jnp.exp(m_old - m_new)
        p = jnp.exp(s - m_new)
        l_ref[...] = alpha * l_ref[...] + jnp.sum(p, axis=1, keepdims=True)
        acc_ref[...] = alpha * acc_ref[...] + jnp.dot(p.astype(BF16), vchunk, preferred_element_type=F32)
        m_ref[...] = m_new

    sel = jnp.where(_sort_key(sc_ref[0]) >= thr, 1.0, 0.0)
    update(sel, jnp.concatenate([k_refs[p][0] for p in range(pages)], axis=0).astype(BF16),
           jnp.concatenate([v_refs[p][0] for p in range(pages)], axis=0).astype(BF16))

    @pl.when(c == nchunks - 1)
    def _():
        t = lax.broadcasted_iota(I32, (rows, LANES), 0)
        j = lax.broadcasted_iota(I32, (rows, LANES), 1)
        seln = jnp.where(jnp.where(j <= t, _sort_key(scn_ref[0]), INT_MIN) >= thr, 1.0, 0.0)
        update(seln, kn_ref[0].astype(BF16), vn_ref[0].astype(BF16))
        out = acc_ref[...] / l_ref[...]
        for h in range(ATT_HEADS):
            o_ref[0, :, h * HEAD_DIM:(h + 1) * HEAD_DIM] = out[h * rows:(h + 1) * rows, :].astype(BF16)


def _sample_attention(page_table, q, sc, thr, scn, kb_new, vb_new, cache_k, cache_v, pages):
    batch, n_pages = page_table.shape
    rows = SUBLANES
    nchunks = n_pages // pages
    seq = lambda b, c, pt: (b, 0, 0)
    page_map = lambda p: functools.partial(lambda b, c, pt, p: (pt[b, c * pages + p], 0, 0), p=p)
    pcols = PAGE_SIZE * KV_HEADS
    k_specs = [pl.BlockSpec((1, pcols, HEAD_DIM), page_map(p)) for p in range(pages)]
    v_specs = [pl.BlockSpec((1, pcols, HEAD_DIM), page_map(p)) for p in range(pages)]
    kern = functools.partial(_sample_attn_kernel, pages=pages)
    grid_spec = pltpu.PrefetchScalarGridSpec(
        num_scalar_prefetch=1,
        grid=(batch, nchunks),
        in_specs=[pl.BlockSpec((1, rows, ATT_Q_W), seq),
                  pl.BlockSpec((1, rows, pages * PAGE_SIZE), lambda b, c, pt: (b, 0, c)),
                  pl.BlockSpec((1, rows, LANES), seq), pl.BlockSpec((1, rows, LANES), seq),
                  pl.BlockSpec((1, pcols, HEAD_DIM), seq), pl.BlockSpec((1, pcols, HEAD_DIM), seq)]
                 + k_specs + v_specs,
        out_specs=pl.BlockSpec((1, rows, ATT_Q_W), seq),
        scratch_shapes=[pltpu.VMEM((ATT_HEADS * rows, HEAD_DIM), F32),
                        pltpu.VMEM((ATT_HEADS * rows, 1), F32),
                        pltpu.VMEM((ATT_HEADS * rows, 1), F32),
                        pltpu.VMEM((ATT_HEADS * rows, HEAD_DIM), F32)],
    )
    return pl.pallas_call(
        kern,
        grid_spec=grid_spec,
        out_shape=jax.ShapeDtypeStruct((batch, rows, ATT_Q_W), BF16),
        compiler_params=_cparams("arbitrary", "arbitrary"),
        name="sample_attention",
    )(page_table, q, sc, thr, scn, kb_new, vb_new, *([cache_k] * pages), *([cache_v] * pages))


def _ssm_kernel(u_ref, s0r_ref, s0i_ref, ar_ref, ai_ref, bre_ref, bim_ref, cre_ref, cim_ref, d_ref,
                wglu_ref, z_ref, sr_out, si_out, sre_ref, sim_ref, str_ref, sti_ref, *, tc, nb, lc, cb):
    @pl.when(pl.program_id(0) == 0)
    def _():
        str_ref[...] = s0r_ref[...]
        sti_ref[...] = s0i_ref[...]

    width = u_ref.shape[-1]
    nst = sre_ref.shape[-1]
    u = u_ref[...].reshape(tc * nb, width)
    ub = u.astype(BF16)
    sb = cb * (nst // width)
    blocks = [(slice(k * cb, (k + 1) * cb), slice(k * sb, (k + 1) * sb)) for k in range(width // cb)]
    for csl, ssl in blocks:
        sre_ref[:, ssl] = jnp.dot(ub[:, csl], bre_ref[csl, ssl], preferred_element_type=F32)
        sim_ref[:, ssl] = jnp.dot(ub[:, csl], bim_ref[csl, ssl], preferred_element_type=F32)

    for ci in range(nst // lc):
        lsl = slice(ci * lc, (ci + 1) * lc)
        ar = jnp.broadcast_to(ar_ref[:, lsl], (nb, lc))
        ai = jnp.broadcast_to(ai_ref[:, lsl], (nb, lc))

        def step(j, carry):
            sr, si = carry
            r0 = pl.multiple_of(j * nb, nb)
            nr = ar * sr - ai * si + sre_ref[pl.ds(r0, nb), lsl]
            ni = ar * si + ai * sr + sim_ref[pl.ds(r0, nb), lsl]
            sre_ref[pl.ds(r0, nb), lsl] = nr
            sim_ref[pl.ds(r0, nb), lsl] = ni
            return nr, ni
        sr, si = lax.fori_loop(0, tc, step, (str_ref[:, lsl], sti_ref[:, lsl]))
        str_ref[:, lsl] = sr
        sti_ref[:, lsl] = si

    y = jnp.concatenate(
        [jnp.dot(sre_ref[:, ssl].astype(BF16), cre_ref[ssl, csl], preferred_element_type=F32)
         - jnp.dot(sim_ref[:, ssl].astype(BF16), cim_ref[ssl, csl], preferred_element_type=F32)
         for csl, ssl in blocks], axis=1) + u * d_ref[...]
    z = _gelu_tanh(y)
    z = z * _sigmoid(jnp.dot(z.astype(BF16), wglu_ref[...], preferred_element_type=F32))
    z_ref[...] = z.reshape(tc, nb, width).astype(BF16)
    sr_out[...] = str_ref[...]
    si_out[...] = sti_ref[...]


def _ssm_branch(u_tm, s0r, s0i, ar, ai, bre, bim, cre, cim, d, wglu, tc):
    t, nb, width = u_tm.shape
    nst = ar.shape[1]
    lc = min(256, nst)
    mxu = 256
    cb = mxu if width % mxu == 0 and (mxu % SSM_GROUP == 0) else width
    kern = functools.partial(_ssm_kernel, tc=tc, nb=nb, lc=lc, cb=cb)
    return pl.pallas_call(
        kern,
        grid=(t // tc,),
        in_specs=[pl.BlockSpec((tc, nb, width), lambda c: (c, 0, 0)),
                  _const_spec((nb, nst)), _const_spec((nb, nst)),
                  _const_spec((1, nst)), _const_spec((1, nst)),
                  _const_spec((width, nst)), _const_spec((width, nst)),
                  _const_spec((nst, width)), _const_spec((nst, width)),
                  _const_spec((1, width)), _const_spec((width, width))],
        out_specs=[pl.BlockSpec((tc, nb, width), lambda c: (c, 0, 0)),
                   pl.BlockSpec((nb, nst), lambda c: (0, 0)), pl.BlockSpec((nb, nst), lambda c: (0, 0))],
        out_shape=[jax.ShapeDtypeStruct((t, nb, width), BF16),
                   jax.ShapeDtypeStruct((nb, nst), F32), jax.ShapeDtypeStruct((nb, nst), F32)],
        scratch_shapes=[pltpu.VMEM((tc * nb, nst), F32), pltpu.VMEM((tc * nb, nst), F32),
                        pltpu.VMEM((nb, nst), F32), pltpu.VMEM((nb, nst), F32)],
        compiler_params=_cparams("arbitrary"),
        name="ssm_branch",
    )(u_tm, s0r, s0i, ar, ai, bre, bim, cre, cim, d, wglu)


MERGE_CW = 512


def _merge_kernel(o_ref, z_ref, x_ref, g_ref, wg_ref, wa_ref, ws_ref, wo_ref, out_ref, mg_ref):
    x = x_ref[...]
    d = x.shape[1]
    h = _rms(x, g_ref[...]).astype(BF16)
    o = o_ref[...]
    z = z_ref[...]
    for c in range(d // MERGE_CW):
        csl = slice(c * MERGE_CW, (c + 1) * MERGE_CW)
        ga = jnp.dot(h, wg_ref[:, csl], preferred_element_type=F32)
        gs = jnp.dot(h, wg_ref[:, d + c * MERGE_CW:d + (c + 1) * MERGE_CW], preferred_element_type=F32)
        a = jnp.dot(o, wa_ref[:, csl], preferred_element_type=F32)
        s = jnp.dot(z, ws_ref[:, csl], preferred_element_type=F32)
        mg_ref[:, csl] = (_sigmoid(ga) * a + _sigmoid(gs) * s).astype(BF16)
    out_ref[...] = x + jnp.dot(mg_ref[...], wo_ref[...], preferred_element_type=F32)


def _merge(o_att, z, z_map, x, g, wg, wa, ws, wo, tm):
    n, d = x.shape
    zw = ws.shape[0]
    row = lambda i: (i, 0)
    return pl.pallas_call(
        _merge_kernel,
        grid=(n // tm,),
        in_specs=[pl.BlockSpec((tm, ATT_Q_W), row), pl.BlockSpec((tm, zw), z_map),
                  pl.BlockSpec((tm, d), row), _const_spec((1, d)), _const_spec(wg.shape),
                  _const_spec(wa.shape), _const_spec(ws.shape), _const_spec(wo.shape)],
        out_specs=pl.BlockSpec((tm, d), row),
        out_shape=jax.ShapeDtypeStruct((n, d), F32),
        scratch_shapes=[pltpu.VMEM((tm, d), BF16)],
        compiler_params=_cparams("arbitrary"),
        name="gated_merge",
    )(o_att, z, x, g, wg, wa, ws, wo)


def _shift_rows(x, head):
    return jnp.concatenate([head, x[:x.shape[0] - head.shape[0], :]], axis=0)


def _ffn_kernel(x_ref, g_ref, init_ref, wup_ref, cw_ref, cb_ref, wdn_ref, out_ref, st_ref,
                h_ref, acc_ref, hist_ref, carry_ref, *, tm, tf, shift, tiles_per_seq):
    i = pl.program_id(0)
    f = pl.program_id(1)

    @pl.when(f == 0)
    def _():
        h_ref[...] = _rms(x_ref[...], g_ref[...]).astype(BF16)
        acc_ref[...] = jnp.zeros(acc_ref.shape, F32)

    first = (i % tiles_per_seq) == 0

    @pl.when(first)
    def _():
        hist_ref[...] = init_ref[...]

    @pl.when(jnp.logical_not(first))
    def _():
        hist_ref[...] = carry_ref[f]

    def conv_piece(csl):
        up = jnp.dot(h_ref[...], wup_ref[:, csl], preferred_element_type=F32)
        hist = hist_ref[:, csl]
        prev1 = _shift_rows(up, hist[shift:, :])
        prev2 = _shift_rows(up, hist)
        conv = (cw_ref[0, 2:3, csl] * up + cw_ref[0, 1:2, csl] * prev1 + cw_ref[0, 0:1, csl] * prev2
                + cb_ref[0, :, csl])
        return conv, up[tm - 2 * shift:, :]

    down = None
    for j in range(tf // FFN_SUB):
        gsl = slice(j * FFN_SUB, (j + 1) * FFN_SUB)
        vsl = slice(tf + j * FFN_SUB, tf + (j + 1) * FFN_SUB)
        gate, gate_tail = conv_piece(gsl)
        val, val_tail = conv_piece(vsl)
        for sl, tail in ((gsl, gate_tail), (vsl, val_tail)):
            carry_ref[f, :, sl] = tail
            st_ref[0, :, sl] = tail
        act = (_gelu_tanh(gate) * val).astype(BF16)
        part = jnp.dot(act, wdn_ref[0, j * FFN_SUB:(j + 1) * FFN_SUB, :], preferred_element_type=F32)
        down = part if down is None else down + part
    acc_ref[...] += down

    @pl.when(f == pl.num_programs(1) - 1)
    def _():
        out_ref[...] = x_ref[...] + acc_ref[...]


def _conv_ffn(x, g, init, wup, cw, cb, wdn, tm, shift, tiles_per_seq):
    n, d = x.shape
    nf, tf, _ = wdn.shape
    tf2 = 2 * tf
    hist = 2 * shift
    kern = functools.partial(_ffn_kernel, tm=tm, tf=tf, shift=shift, tiles_per_seq=tiles_per_seq)
    return pl.pallas_call(
        kern,
        grid=(n // tm, nf),
        in_specs=[pl.BlockSpec((tm, d), lambda i, f: (i, 0)), _const_spec((1, d)),
                  pl.BlockSpec((hist, tf2), lambda i, f: (0, f)),
                  pl.BlockSpec((d, tf2), lambda i, f: (0, f)),
                  pl.BlockSpec((1, CONV_W, tf2), lambda i, f: (f, 0, 0)),
                  pl.BlockSpec((1, 1, tf2), lambda i, f: (f, 0, 0)),
                  pl.BlockSpec((1, tf, d), lambda i, f: (f, 0, 0))],
        out_specs=[pl.BlockSpec((tm, d), lambda i, f: (i, 0)),
                   pl.BlockSpec((1, hist, tf2), lambda i, f: (i, 0, f))],
        out_shape=[jax.ShapeDtypeStruct((n, d), F32),
                   jax.ShapeDtypeStruct((n // tm, hist, nf * tf2), F32)],
        scratch_shapes=[pltpu.VMEM((tm, d), BF16), pltpu.VMEM((tm, d), F32),
                        pltpu.VMEM((hist, tf2), F32), pltpu.VMEM((nf, hist, tf2), F32)],
        compiler_params=_cparams("arbitrary", "arbitrary"),
        name="conv_ffn",
    )(x, g, init, wup, cw, cb, wdn)


def _ple_kernel(x_ref, p_ref, gpg_ref, wpg_ref, wple_ref, gple_ref, out_ref):
    x = x_ref[...]
    ple = _rms(jnp.dot(p_ref[...].astype(BF16), wple_ref[...], preferred_element_type=F32), gple_ref[...])
    gate = _sigmoid(jnp.dot(_rms(x, gpg_ref[...]).astype(BF16), wpg_ref[...], preferred_element_type=F32))
    out_ref[...] = x + gate * ple


def _ple(x, p, gpg, wpg, wple, gple, tm):
    n, d = x.shape
    pd = p.shape[1]
    row = lambda i: (i, 0)
    return pl.pallas_call(
        _ple_kernel,
        grid=(n // tm,),
        in_specs=[pl.BlockSpec((tm, d), row), pl.BlockSpec((tm, pd), row), _const_spec((1, d)),
                  _const_spec((d, d)), _const_spec((pd, d)), _const_spec((1, d))],
        out_specs=pl.BlockSpec((tm, d), row),
        out_shape=jax.ShapeDtypeStruct((n, d), F32),
        compiler_params=_cparams("arbitrary"),
        name="ple_gate",
    )(x, p, gpg, wpg, wple, gple)


def _rope_tables(pos, head_dim, rot_dim):
    half = rot_dim // 2
    inv_freq = jnp.asarray(
        np.float32(ROPE_THETA) ** (-np.arange(half, dtype=np.float32) / np.float32(half)))
    ang = pos.astype(F32)[:, None] * inv_freq[None, :]
    cos, sin = jnp.cos(ang), jnp.sin(ang)
    n = pos.shape[0]
    ones = jnp.ones((n, head_dim - rot_dim), F32)
    zeros = jnp.zeros((n, head_dim - rot_dim), F32)
    zh = jnp.zeros((n, half), F32)
    c = jnp.concatenate([cos, cos, ones], axis=1)
    sa = jnp.concatenate([zh, sin, zeros], axis=1)
    sb = jnp.concatenate([-sin, zh, zeros], axis=1)
    reps = LANES // head_dim
    return jnp.stack([jnp.tile(t, (1, reps)) for t in (c, sa, sb)], axis=0)


def _ssm_params(lw):
    a_re = lw['ssm_a_re'].astype(F32)
    a_im = lw['ssm_a_im'].astype(F32)
    groups, nstate = a_re.shape
    dt = jnp.exp(lw['ssm_log_dt'].astype(F32))[:, None]
    mag = jnp.exp(a_re * dt)
    ang = a_im * dt
    ab_re, ab_im = mag * jnp.cos(ang), mag * jnp.sin(ang)
    den = a_re * a_re + a_im * a_im
    f_re = ((ab_re - 1.0) * a_re + ab_im * a_im) / den
    f_im = (ab_im * a_re - (ab_re - 1.0) * a_im) / den
    b_re = lw['ssm_b_re'].astype(F32)
    b_im = lw['ssm_b_im'].astype(F32)
    bb_re = f_re[..., None] * b_re - f_im[..., None] * b_im
    bb_im = f_re[..., None] * b_im + f_im[..., None] * b_re
    eye = jnp.eye(groups, dtype=F32)

    def in_map(bb):
        m = jnp.einsum('gnc,gh->gchn', bb, eye)
        return m.reshape(groups * SSM_GROUP, groups * nstate).astype(BF16)

    def out_map(cc):
        m = jnp.einsum('gcn,gh->gnhc', cc.astype(F32), eye)
        return m.reshape(groups * nstate, groups * SSM_GROUP).astype(BF16)

    return dict(ar=ab_re.reshape(1, -1), ai=ab_im.reshape(1, -1),
                bre=in_map(bb_re), bim=in_map(bb_im),
                cre=out_map(lw['ssm_c_re']), cim=out_map(lw['ssm_c_im']),
                d=lw['ssm_d'].astype(F32).reshape(1, -1), wglu=lw['w_glu'].astype(BF16))


def _chunk_features(a, d_ff, tf):
    nf = -(-d_ff // tf)
    pad = [(0, 0)] * (a.ndim - 1) + [(0, nf * tf - d_ff)]
    gate = jnp.pad(a[..., :d_ff], pad).reshape(a.shape[:-1] + (nf, tf))
    val = jnp.pad(a[..., d_ff:], pad).reshape(a.shape[:-1] + (nf, tf))
    return jnp.concatenate([gate, val], axis=-1)


def _unchunk_features(a, d_ff, tf):
    nf = a.shape[-1] // (2 * tf)
    a = a.reshape(a.shape[:-1] + (nf, 2, tf))
    gate = a[..., 0, :].reshape(a.shape[:-3] + (nf * tf,))[..., :d_ff]
    val = a[..., 1, :].reshape(a.shape[:-3] + (nf * tf,))[..., :d_ff]
    return jnp.concatenate([gate, val], axis=-1)


FFN_TF = 512
FFN_SUB = 256


def _layer_weights(lw):
    w_in = lw['w_in']
    d = w_in.shape[0]
    o = np.cumsum([0, ATT_Q_W, ATT_KV_W, ATT_KV_W, IDX_Q_W, IDX_DIM, IDX_HEADS])
    ssm_w = lw['ssm_d'].shape[0]
    w_q_kv = w_in[:, :o[3]]
    w_iq, w_ik, w_iw = w_in[:, o[3]:o[4]], w_in[:, o[4]:o[5]], w_in[:, o[5]:o[6]]
    w_u = w_in[:, o[6]:o[6] + ssm_w]
    w_gates = w_in[:, o[6] + ssm_w:]
    w_idx = jnp.concatenate([w_iq, w_ik, w_ik, w_iw, jnp.zeros((d, LANES - IDX_HEADS), w_in.dtype)], axis=1)
    d_ff = lw['w_down'].shape[0]
    nf = -(-d_ff // FFN_TF)
    wup = _chunk_features(lw['w_up'], d_ff, FFN_TF).reshape(d, nf * 2 * FFN_TF).astype(BF16)
    wdn = jnp.pad(lw['w_down'], ((0, nf * FFN_TF - d_ff), (0, 0))).reshape(nf, FFN_TF, d).astype(BF16)
    cw = jnp.transpose(_chunk_features(lw['conv_w'], d_ff, FFN_TF), (1, 0, 2)).astype(F32)
    cb = _chunk_features(lw['conv_b'], d_ff, FFN_TF).reshape(nf, 1, 2 * FFN_TF).astype(F32)
    out = dict(
        g_mix=lw['g_mix'].reshape(1, d), g_q=lw['g_q'].reshape(1, HEAD_DIM), g_k=lw['g_k'].reshape(1, HEAD_DIM),
        w_qkv=w_q_kv.astype(BF16), w_idx=w_idx.astype(BF16), w_u=w_u.astype(BF16),
        w_gates=w_gates.astype(BF16),
        w_att_br=lw['w_att_br'].astype(BF16), w_ssm_br=lw['w_ssm_br'].astype(BF16), w_o=lw['w_o'].astype(BF16),
        g_ffn=lw['g_ffn'].reshape(1, d), wup=wup, wdn=wdn, cw=cw, cb=cb, d_ff=d_ff,
        g_pg=lw['g_pg'].reshape(1, d), w_pg=lw['w_pg'].astype(BF16), w_ple=lw['w_ple'].astype(BF16),
        g_ple=lw['g_ple'].reshape(1, d))
    out.update(_ssm_params(lw))
    return out


def _row_tile(n, cap):
    return cap if n % cap == 0 else n


def _prompt_layer(x3, p3, w):
    batch, seq, d = x3.shape
    n = batch * seq
    x = x3.reshape(n, d)
    tm = _row_tile(seq, 512)
    tpb = seq // tm
    pos = jnp.arange(seq)
    tab_att = _rope_tables(pos, HEAD_DIM, ROT_DIM)
    tab_idx = _rope_tables(pos, IDX_DIM, IDX_ROT_DIM)

    ssm_w = w['w_u'].shape[1]
    q, k, v, kb, vb, iq, ik2, ikb, iw, u_tm = _in_proj(
        x, w['g_mix'], w['w_qkv'], w['w_idx'], w['w_u'], w['g_q'], w['g_k'], tab_att, tab_idx, tm,
        (seq, batch * ssm_w), lambda i: (i % tpb, i // tpb))
    u_tm = u_tm.reshape(seq, batch, ssm_w)

    o_att = _prompt_attention(q, iq, iw, ikb, kb, vb, batch, seq)

    nst = w['ar'].shape[1]
    zeros = jnp.zeros((batch, nst), F32)
    z_tm, s_re, s_im = _ssm_branch(u_tm, zeros, zeros, w['ar'], w['ai'], w['bre'], w['bim'],
                                   w['cre'], w['cim'], w['d'], w['wglu'], tc=min(32, seq))
    tmm = _row_tile(seq, 256)
    tpbm = seq // tmm
    x1 = _merge(o_att, z_tm.reshape(seq, batch * ssm_w), lambda i: (i % tpbm, i // tpbm),
                x, w['g_mix'], w['w_gates'], w['w_att_br'], w['w_ssm_br'], w['w_o'], tmm)

    nf, tf2 = w['wdn'].shape[0], 2 * w['wdn'].shape[1]
    x2, conv = _conv_ffn(x1, w['g_ffn'], jnp.zeros((CONV_W - 1, nf * tf2), F32),
                         w['wup'], w['cw'], w['cb'], w['wdn'], tm, 1, tpb)
    conv = _unchunk_features(conv[tpb - 1::tpb], w['d_ff'], tf2 // 2)
    y = _ple(x2, p3.reshape(n, -1), w['g_pg'], w['w_pg'], w['w_ple'], w['g_ple'], tm)

    groups = nst // SSM_STATE
    return (y.reshape(batch, seq, d), k.reshape(batch, seq, KV_HEADS, HEAD_DIM),
            v.reshape(batch, seq, KV_HEADS, HEAD_DIM), ik2[:, :IDX_DIM].reshape(batch, seq, IDX_DIM),
            s_re.reshape(batch, groups, SSM_STATE), s_im.reshape(batch, groups, SSM_STATE), conv)


def _pad_rows(a, rows):
    pad = [(0, 0)] * a.ndim
    pad[1] = (0, rows - a.shape[1])
    return jnp.pad(a, pad)


def _sample_layer(x3, p3, cache_k, cache_v, cache_ik, s0_re, s0_im, conv0, page_table, w):
    batch, t_new, d = x3.shape
    n = batch * t_new
    n_pages = page_table.shape[1]
    past = n_pages * PAGE_SIZE
    x = x3.reshape(n, d)
    pos = past + (jnp.arange(n) % t_new)
    tab_att = _rope_tables(pos, HEAD_DIM, ROT_DIM)
    tab_idx = _rope_tables(pos, IDX_DIM, IDX_ROT_DIM)

    ssm_w = w['w_u'].shape[1]
    q, k, v, kb, vb, iq, ik2, ikb, iw, u = _in_proj(
        x, w['g_mix'], w['w_qkv'], w['w_idx'], w['w_u'], w['g_q'], w['g_k'], tab_att, tab_idx, n,
        (n, ssm_w), lambda i: (i, 0))

    per_seq = lambda a: a.reshape(batch, t_new, a.shape[-1])
    pages = max(p for p in (32, 16, 8, 4, 2, 1) if n_pages % p == 0)
    sc, thr, scn = _sample_scores(
        page_table, _pad_rows(per_seq(iq), SUBLANES), _pad_rows(per_seq(iw), SUBLANES),
        jnp.swapaxes(_pad_rows(per_seq(ikb[:, :IDX_DIM]), LANES), 1, 2), cache_ik, pages, t_new)
    new_page = lambda a: _pad_rows(a.reshape(batch, t_new * KV_HEADS, HEAD_DIM), PAGE_SIZE * KV_HEADS)
    o_att = _sample_attention(
        page_table, _pad_rows(per_seq(q), SUBLANES), sc, thr, scn,
        new_page(k), new_page(v), cache_k, cache_v, pages)
    o_att = o_att[:, :t_new].reshape(n, ATT_Q_W)

    nst = w['ar'].shape[1]
    u_tm = jnp.transpose(u.reshape(batch, t_new, ssm_w), (1, 0, 2))
    z_tm, s_re, s_im = _ssm_branch(u_tm, s0_re.reshape(batch, nst).astype(F32),
                                   s0_im.reshape(batch, nst).astype(F32), w['ar'], w['ai'], w['bre'],
                                   w['bim'], w['cre'], w['cim'], w['d'], w['wglu'], tc=t_new)
    z = jnp.transpose(z_tm, (1, 0, 2)).reshape(n, ssm_w)
    x1 = _merge(o_att, z, lambda i: (i, 0), x, w['g_mix'], w['w_gates'], w['w_att_br'], w['w_ssm_br'],
                w['w_o'], n)

    nf, tf2 = w['wdn'].shape[0], 2 * w['wdn'].shape[1]
    d_ff = w['d_ff']
    x1_tm = jnp.transpose(x1.reshape(batch, t_new, d), (1, 0, 2)).reshape(n, d)
    init = _chunk_features(jnp.transpose(conv0.astype(F32), (1, 0, 2)), d_ff, tf2 // 2)
    init = init.reshape((CONV_W - 1) * batch, nf * tf2)
    x2_tm, conv = _conv_ffn(x1_tm, w['g_ffn'], init, w['wup'], w['cw'], w['cb'], w['wdn'], n, batch, 1)
    conv = _unchunk_features(conv.reshape(CONV_W - 1, batch, nf * tf2), d_ff, tf2 // 2)
    conv = jnp.transpose(conv, (1, 0, 2))
    x2 = jnp.transpose(x2_tm.reshape(t_new, batch, d), (1, 0, 2)).reshape(n, d)
    y = _ple(x2, p3.reshape(n, -1), w['g_pg'], w['w_pg'], w['w_ple'], w['g_ple'], n)

    groups = nst // SSM_STATE
    return (y.reshape(batch, t_new, d), k.reshape(batch, t_new, KV_HEADS, HEAD_DIM),
            v.reshape(batch, t_new, KV_HEADS, HEAD_DIM), ik2[:, :IDX_DIM].reshape(batch, t_new, IDX_DIM),
            s_re.reshape(batch, groups, SSM_STATE), s_im.reshape(batch, groups, SSM_STATE), conv)


def kernel(x_prompt, x_sample, cache_k, cache_v, cache_idx_k, state_ssm_re, state_ssm_im, state_conv, page_table, p_prompt, p_sample, g_mix, w_in, g_q, g_k, ssm_a_re, ssm_a_im, ssm_log_dt, ssm_b_re, ssm_b_im, ssm_c_re, ssm_c_im, ssm_d, w_glu, w_att_br, w_ssm_br, w_o, g_ffn, w_up, conv_w, conv_b, w_down, w_ple, g_ple, g_pg, w_pg):
    depth = w_in.shape[0]
    n_pool = cache_k.shape[1]
    pool_k = cache_k.reshape(depth * n_pool, PAGE_SIZE * KV_HEADS, HEAD_DIM)
    pool_v = cache_v.reshape(depth * n_pool, PAGE_SIZE * KV_HEADS, HEAD_DIM)
    pool_ik = jnp.swapaxes(cache_idx_k, 2, 3).reshape(depth * n_pool, IDX_DIM, PAGE_SIZE)
    xp, xs = x_prompt, x_sample
    outs_p, outs_s = [], []
    for i in range(depth):
        lw = dict(g_mix=g_mix[i], w_in=w_in[i], g_q=g_q[i], g_k=g_k[i],
                  ssm_a_re=ssm_a_re[i], ssm_a_im=ssm_a_im[i], ssm_log_dt=ssm_log_dt[i],
                  ssm_b_re=ssm_b_re[i], ssm_b_im=ssm_b_im[i], ssm_c_re=ssm_c_re[i], ssm_c_im=ssm_c_im[i],
                  ssm_d=ssm_d[i], w_glu=w_glu[i], w_att_br=w_att_br[i], w_ssm_br=w_ssm_br[i], w_o=w_o[i],
                  g_ffn=g_ffn[i], w_up=w_up[i], conv_w=conv_w[i], conv_b=conv_b[i], w_down=w_down[i],
                  w_ple=w_ple[i], g_ple=g_ple[i], g_pg=g_pg[i], w_pg=w_pg[i])
        w = _layer_weights(lw)
        xp, *rest_p = _prompt_layer(xp, p_prompt[i], w)
        outs_p.append(rest_p)
        xs, *rest_s = _sample_layer(xs, p_sample[i], pool_k, pool_v, pool_ik, state_ssm_re[i],
                                    state_ssm_im[i], state_conv[i], page_table + i * n_pool, w)
        outs_s.append(rest_s)
    stack = lambda outs, j: jnp.stack([o[j] for o in outs], 0)
    return ((xp, xs) + tuple(stack(outs_p, j) for j in range(6))
            + tuple(stack(outs_s, j) for j in range(6)))
```

```python
import functools

import numpy as np
import jax
import jax.numpy as jnp
from jax import lax
from jax.experimental import pallas as pl
from jax.experimental.pallas import tpu as pltpu

F32 = jnp.float32
BF16 = jnp.bfloat16
I32 = jnp.int32

ATT_HEADS = 8
KV_HEADS = 2
HEAD_DIM = 128
GROUP = ATT_HEADS // KV_HEADS
ROT_DIM = HEAD_DIM // 4
ROPE_THETA = 500000.0
IDX_HEADS = 16
IDX_DIM = 64
IDX_ROT_DIM = IDX_DIM // 4
IDX_SCALE = (IDX_HEADS ** -0.5) * (IDX_DIM ** -0.5)
TOPK_MAX = 256
Q_BLOCK = 128
PAGE_SIZE = 128
SSM_GROUP = 16
SSM_STATE = 64
CONV_W = 3
EPS = 1e-6

LANES = 128
SUBLANES = 8
NEG = -1e30
INT_MIN = -2 ** 31
VMEM_LIMIT = 56 * 1024 * 1024

ATT_Q_W = ATT_HEADS * HEAD_DIM
ATT_KV_W = KV_HEADS * HEAD_DIM
IDX_Q_W = IDX_HEADS * IDX_DIM


def _cparams(*sem):
    return pltpu.CompilerParams(dimension_semantics=sem, vmem_limit_bytes=VMEM_LIMIT)


def _const_spec(shape):
    nd = len(shape)
    return pl.BlockSpec(shape, lambda *_: (0,) * nd, pipeline_mode=pl.Buffered(1))


def _rms(x, g):
    ms = jnp.mean(x * x, axis=-1, keepdims=True)
    return (x * lax.rsqrt(ms + EPS)) * g


def _gelu_tanh(x):
    return 0.5 * x * (1.0 + jnp.tanh(0.7978845608028654 * (x + 0.044715 * (x * x * x))))


def _sigmoid(x):
    return 1.0 / (1.0 + jnp.exp(-x))


def _rope(x, tab_ref, half):
    return (x * tab_ref[0] + pltpu.roll(x, half, 1) * tab_ref[1]
            + pltpu.roll(x, LANES - half, 1) * tab_ref[2])


def _dot_t(a, b):
    return lax.dot_general(a, b, (((1,), (1,)), ((), ())), preferred_element_type=F32)


def _in_proj_kernel(x_ref, g_ref, wq_ref, wi_ref, wu_ref, gq_ref, gk_ref, ta_ref, ti_ref,
                    q_ref, k_ref, v_ref, kb_ref, vb_ref, iq_ref, ik_ref, ikb_ref, iw_ref, u_ref):
    tm = x_ref.shape[0]
    h = _rms(x_ref[...], g_ref[...]).astype(BF16)

    y = jnp.dot(h, wq_ref[...], preferred_element_type=F32)
    half = ROT_DIM // 2
    for hh in range(ATT_HEADS):
        sl = slice(hh * HEAD_DIM, (hh + 1) * HEAD_DIM)
        qh = _rope(_rms(y[:, sl], gq_ref[...]), ta_ref, half)
        q_ref[:, sl] = (qh * (HEAD_DIM ** -0.5)).astype(BF16)
    for hh in range(KV_HEADS):
        sl = slice(hh * HEAD_DIM, (hh + 1) * HEAD_DIM)
        kh = _rope(_rms(y[:, ATT_Q_W + hh * HEAD_DIM:ATT_Q_W + (hh + 1) * HEAD_DIM], gk_ref[...]),
                   ta_ref, half)
        vh = y[:, ATT_Q_W + ATT_KV_W + hh * HEAD_DIM:ATT_Q_W + ATT_KV_W + (hh + 1) * HEAD_DIM]
        k_ref[pl.ds(hh, tm, stride=KV_HEADS), :] = kh
        v_ref[pl.ds(hh, tm, stride=KV_HEADS), :] = vh
        kb_ref[:, sl] = kh.astype(BF16)
        vb_ref[:, sl] = vh.astype(BF16)

    y = jnp.dot(h, wi_ref[...], preferred_element_type=F32)
    half = IDX_ROT_DIM // 2
    for s in range(IDX_Q_W // LANES):
        sl = slice(s * LANES, (s + 1) * LANES)
        iq_ref[:, sl] = _rope(y[:, sl], ti_ref, half).astype(BF16)
    ik2 = _rope(y[:, IDX_Q_W:IDX_Q_W + LANES], ti_ref, half)
    ik_ref[...] = ik2
    ikb_ref[...] = ik2.astype(BF16)
    iw_ref[...] = y[:, IDX_Q_W + LANES:] * IDX_SCALE

    u_ref[...] = jnp.dot(h, wu_ref[...], preferred_element_type=F32)


def _in_proj(x, g, wq, wi, wu, gq, gk, tab_att, tab_idx, tm, u_shape, u_map):
    n, d = x.shape
    uw = wu.shape[1]
    p_blocks = tab_att.shape[1] // tm
    row = lambda i: (i, 0)
    tab = lambda i: (0, i % p_blocks, 0)
    return pl.pallas_call(
        _in_proj_kernel,
        grid=(n // tm,),
        in_specs=[pl.BlockSpec((tm, d), row), _const_spec((1, d)), _const_spec(wq.shape), _const_spec(wi.shape),
                  _const_spec(wu.shape), _const_spec((1, HEAD_DIM)), _const_spec((1, HEAD_DIM)),
                  pl.BlockSpec((3, tm, LANES), tab), pl.BlockSpec((3, tm, LANES), tab)],
        out_specs=[pl.BlockSpec((tm, ATT_Q_W), row), pl.BlockSpec((KV_HEADS * tm, HEAD_DIM), row),
                   pl.BlockSpec((KV_HEADS * tm, HEAD_DIM), row), pl.BlockSpec((tm, ATT_KV_W), row),
                   pl.BlockSpec((tm, ATT_KV_W), row), pl.BlockSpec((tm, IDX_Q_W), row),
                   pl.BlockSpec((tm, LANES), row), pl.BlockSpec((tm, LANES), row), pl.BlockSpec((tm, LANES), row),
                   pl.BlockSpec((tm, uw), u_map)],
        out_shape=[jax.ShapeDtypeStruct((n, ATT_Q_W), BF16),
                   jax.ShapeDtypeStruct((KV_HEADS * n, HEAD_DIM), F32),
                   jax.ShapeDtypeStruct((KV_HEADS * n, HEAD_DIM), F32),
                   jax.ShapeDtypeStruct((n, ATT_KV_W), BF16), jax.ShapeDtypeStruct((n, ATT_KV_W), BF16),
                   jax.ShapeDtypeStruct((n, IDX_Q_W), BF16), jax.ShapeDtypeStruct((n, LANES), F32),
                   jax.ShapeDtypeStruct((n, LANES), BF16), jax.ShapeDtypeStruct((n, LANES), F32),
                   jax.ShapeDtypeStruct(u_shape, F32)],
        compiler_params=_cparams("arbitrary"),
        name="in_proj",
    )(x, g, wq, wi, wu, gq, gk, tab_att, tab_idx)


def _sort_key(sc):
    bits = pltpu.bitcast(sc, I32)
    return bits ^ ((bits >> 31) & jnp.int32(0x7FFFFFFF))


def _kth_largest(count_ge, shape, topk):
    def body(it, t):
        cand = t + lax.shift_left(jnp.int32(1), 31 - it)
        return jnp.where(count_ge(cand) >= topk, cand, t)
    t = lax.fori_loop(0, 32, body, jnp.full(shape, INT_MIN, I32))
    return jnp.maximum(t, INT_MIN + 1)


def _tree(op, parts):
    while len(parts) > 1:
        nxt = [op(parts[j], parts[j + 1]) for j in range(0, len(parts) - 1, 2)]
        if len(parts) % 2:
            nxt.append(parts[-1])
        parts = nxt
    return parts[0]


def _row_groups(op, x):
    return _tree(op, [x[r:r + SUBLANES, :] for r in range(0, x.shape[0], SUBLANES)])


def _indexer_operands(iq, iw, lhs_ref, wb_ref, rows):
    for h in range(IDX_HEADS):
        lhs_ref[h * rows:(h + 1) * rows, :] = iq[:, h * IDX_DIM:(h + 1) * IDX_DIM].astype(F32)
        wb_ref[h * rows:(h + 1) * rows, :] = jnp.broadcast_to(iw[:, h:h + 1], (rows, LANES))


def _indexer_scores(lhs_ref, wb_ref, keys_t, rows):
    n_keys = keys_t.shape[1]
    r = jnp.dot(lhs_ref[...].astype(BF16), keys_t, preferred_element_type=F32)
    sc = jnp.zeros((rows, n_keys), F32)
    for h in range(IDX_HEADS):
        wh = jnp.concatenate([wb_ref[h * rows:(h + 1) * rows, :]] * (n_keys // LANES), axis=1)
        sc = sc + wh * jnp.maximum(r[h * rows:(h + 1) * rows, :], 0.0)
    return sc


def _prompt_attn_kernel(q_ref, iq_ref, iw_ref, ikb_ref, kb_ref, vb_ref, o_ref,
                        lhs_ref, q2_ref, vt_ref, key_ref, s_ref, m_ref, l_ref, acc_ref, *, seq, kc, topk):
    qb = Q_BLOCK
    npair = ATT_HEADS // 2
    i = pl.program_id(1)
    t0 = i * qb
    nch = (t0 + qb + kc - 1) // kc

    @pl.when(i == 0)
    def _():
        def transpose_chunk(c, carry):
            c0 = pl.multiple_of(c * kc, kc)
            vt_ref[c] = vb_ref[pl.ds(c0, kc), :].astype(F32).T.astype(BF16)
            return carry
        lax.fori_loop(0, seq // kc, transpose_chunk, 0)

    lane = lax.broadcasted_iota(I32, (qb, LANES), 1)
    for s in range(IDX_Q_W // LANES):
        x = iq_ref[:, s * LANES:(s + 1) * LANES].astype(F32)
        lhs_ref[(2 * s) * qb:(2 * s + 1) * qb, :] = jnp.where(lane < IDX_DIM, x, 0.0).astype(BF16)
        lhs_ref[(2 * s + 1) * qb:(2 * s + 2) * qb, :] = jnp.where(lane >= IDX_DIM, x, 0.0).astype(BF16)
    iwt = iw_ref[...].T
    for p in range(npair):
        for r in range(2):
            hsl = slice((2 * p + r) * HEAD_DIM, (2 * p + r + 1) * HEAD_DIM)
            q2_ref[p, r * qb:(r + 1) * qb, :] = q_ref[:, hsl]

    key_pos = lax.broadcasted_iota(I32, (kc, LANES), 0)
    q_pos = t0 + lax.broadcasted_iota(I32, (kc, LANES), 1)

    def score_chunk(c, carry):
        c0 = pl.multiple_of(c * kc, kc)
        ikc = ikb_ref[pl.ds(c0, kc), :]
        sc = jnp.zeros((kc, LANES), F32)
        for s in range(IDX_HEADS // 2):
            r = _dot_t(ikc, lhs_ref[2 * s * qb:(2 * s + 2) * qb, :])
            sc = (sc + iwt[2 * s:2 * s + 1, :] * jnp.maximum(r[:, :LANES], 0.0)
                  + iwt[2 * s + 1:2 * s + 2, :] * jnp.maximum(r[:, LANES:], 0.0))
        key_ref[c] = jnp.where(c0 + key_pos <= q_pos, _sort_key(sc), INT_MIN)
        return carry
    lax.fori_loop(0, nch, score_chunk, 0)

    def count_ge(cand):
        def chunk(c, acc):
            return acc + _row_groups(jnp.add, jnp.where(key_ref[c] >= cand, 1.0, 0.0))
        acc = lax.fori_loop(0, nch, chunk, jnp.zeros((SUBLANES, LANES), F32))
        return jnp.sum(acc, axis=0, keepdims=True)
    thr = _kth_largest(count_ge, (1, LANES), float(topk))

    m_ref[...] = jnp.full(m_ref.shape, NEG, F32)
    l_ref[...] = jnp.zeros(l_ref.shape, F32)
    acc_ref[...] = jnp.zeros(acc_ref.shape, F32)

    def score_pass(c, carry):
        c0 = pl.multiple_of(c * kc, kc)
        sel = key_ref[c] >= thr
        for p in range(npair):
            gsl = slice((2 * p) // GROUP * HEAD_DIM, ((2 * p) // GROUP + 1) * HEAD_DIM)
            s = _dot_t(kb_ref[pl.ds(c0, kc), gsl], q2_ref[p])
            s = jnp.concatenate([jnp.where(sel, s[:, :LANES], NEG), jnp.where(sel, s[:, LANES:], NEG)], axis=1)
            s_ref[c, p] = s
            m_ref[p] = jnp.maximum(m_ref[p], _row_groups(jnp.maximum, s))
        return carry
    lax.fori_loop(0, nch, score_pass, 0)
    m = [jnp.max(m_ref[p], axis=0, keepdims=True) for p in range(npair)]

    def value_pass(c, carry):
        for p in range(npair):
            gsl = slice((2 * p) // GROUP * HEAD_DIM, ((2 * p) // GROUP + 1) * HEAD_DIM)
            pr = jnp.exp(s_ref[c, p] - m[p])
            l_ref[p] += _row_groups(jnp.add, pr)
            acc_ref[p] += jnp.dot(vt_ref[c, gsl, :], pr.astype(BF16),
                                  preferred_element_type=F32)
        return carry
    lax.fori_loop(0, nch, value_pass, 0)

    for p in range(npair):
        o_t = acc_ref[p] / jnp.sum(l_ref[p], axis=0, keepdims=True)
        for r in range(2):
            hsl = slice((2 * p + r) * HEAD_DIM, (2 * p + r + 1) * HEAD_DIM)
            o_ref[:, hsl] = o_t[:, r * qb:(r + 1) * qb].T.astype(BF16)


def _prompt_attention(q, iq, iw, ikb, kb, vb, batch, seq):
    kc = min(512, seq)
    topk = min(TOPK_MAX, seq // 4)
    nq = seq // Q_BLOCK
    blk = lambda b, i: (b * nq + i, 0)
    full = lambda b, i: (b, 0)
    kern = functools.partial(_prompt_attn_kernel, seq=seq, kc=kc, topk=topk)
    return pl.pallas_call(
        kern,
        grid=(batch, nq),
        in_specs=[pl.BlockSpec((Q_BLOCK, ATT_Q_W), blk), pl.BlockSpec((Q_BLOCK, IDX_Q_W), blk),
                  pl.BlockSpec((Q_BLOCK, LANES), blk), pl.BlockSpec((seq, LANES), full),
                  pl.BlockSpec((seq, ATT_KV_W), full), pl.BlockSpec((seq, ATT_KV_W), full)],
        out_specs=pl.BlockSpec((Q_BLOCK, ATT_Q_W), blk),
        out_shape=jax.ShapeDtypeStruct((batch * seq, ATT_Q_W), BF16),
        scratch_shapes=[pltpu.VMEM((IDX_HEADS * Q_BLOCK, LANES), BF16),
                        pltpu.VMEM((ATT_HEADS // 2, 2 * Q_BLOCK, HEAD_DIM), BF16),
                        pltpu.VMEM((seq // kc, ATT_KV_W, kc), BF16),
                        pltpu.VMEM((seq // kc, kc, Q_BLOCK), I32),
                        pltpu.VMEM((seq // kc, ATT_HEADS // 2, kc, 2 * Q_BLOCK), F32),
                        pltpu.VMEM((ATT_HEADS // 2, SUBLANES, 2 * Q_BLOCK), F32),
                        pltpu.VMEM((ATT_HEADS // 2, SUBLANES, 2 * Q_BLOCK), F32),
                        pltpu.VMEM((ATT_HEADS // 2, HEAD_DIM, 2 * Q_BLOCK), F32)],
        compiler_params=_cparams("arbitrary", "arbitrary"),
        name="prompt_attention",
    )(q, iq, iw, ikb, kb, vb)


def _sample_score_kernel(pt_ref, iq_ref, iw_ref, ikn_ref, *rest, pages, n_new):
    page_refs = rest[:pages]
    sc_ref, thr_ref, scn_ref, lhs_ref, wb_ref, key_ref = rest[pages:]
    del pt_ref
    c = pl.program_id(1)
    nchunks = pl.num_programs(1)
    rows = SUBLANES

    @pl.when(c == 0)
    def _():
        _indexer_operands(iq_ref[0], iw_ref[0], lhs_ref, wb_ref, rows)

    keys_t = jnp.concatenate([page_refs[p][0] for p in range(pages)], axis=1).astype(BF16)
    sc = _indexer_scores(lhs_ref, wb_ref, keys_t, rows)
    sc_ref[0] = sc
    key = _sort_key(sc)
    for p in range(pages):
        key_ref[c * pages + p] = key[:, p * PAGE_SIZE:(p + 1) * PAGE_SIZE]

    @pl.when(c == nchunks - 1)
    def _():
        scn = _indexer_scores(lhs_ref, wb_ref, ikn_ref[0], rows)
        scn_ref[0] = scn
        t = lax.broadcasted_iota(I32, (rows, LANES), 0)
        j = lax.broadcasted_iota(I32, (rows, LANES), 1)
        keyn = jnp.where(j <= t, _sort_key(scn), INT_MIN)
        total = key_ref.shape[0]

        def count_ge(cand):
            hit = _tree(jnp.add, [jnp.where(key_ref[pg] >= cand, 1.0, 0.0) for pg in range(total)])
            return jnp.sum(hit + jnp.where(keyn >= cand, 1.0, 0.0), axis=1, keepdims=True)
        topk = float(min(TOPK_MAX, (total * PAGE_SIZE + n_new) // 4))
        thr = _kth_largest(count_ge, (rows, 1), topk)
        thr_ref[0] = jnp.broadcast_to(thr, (rows, LANES))


def _sample_scores(page_table, iq, iw, ikb_new, cache_ik, pages, n_new):
    batch, n_pages = page_table.shape
    rows = SUBLANES
    past = n_pages * PAGE_SIZE
    nchunks = n_pages // pages
    seq = lambda b, c, pt: (b, 0, 0)
    page_specs = [pl.BlockSpec((1, IDX_DIM, PAGE_SIZE),
                               functools.partial(lambda b, c, pt, p: (pt[b, c * pages + p], 0, 0), p=p))
                  for p in range(pages)]
    kern = functools.partial(_sample_score_kernel, pages=pages, n_new=n_new)
    grid_spec = pltpu.PrefetchScalarGridSpec(
        num_scalar_prefetch=1,
        grid=(batch, nchunks),
        in_specs=[pl.BlockSpec((1, rows, IDX_Q_W), seq), pl.BlockSpec((1, rows, LANES), seq),
                  pl.BlockSpec((1, IDX_DIM, LANES), seq)] + page_specs,
        out_specs=[pl.BlockSpec((1, rows, pages * PAGE_SIZE), lambda b, c, pt: (b, 0, c)),
                   pl.BlockSpec((1, rows, LANES), seq), pl.BlockSpec((1, rows, LANES), seq)],
        scratch_shapes=[pltpu.VMEM((IDX_HEADS * rows, IDX_DIM), F32),
                        pltpu.VMEM((IDX_HEADS * rows, LANES), F32),
                        pltpu.VMEM((n_pages, rows, PAGE_SIZE), I32)],
    )
    return pl.pallas_call(
        kern,
        grid_spec=grid_spec,
        out_shape=[jax.ShapeDtypeStruct((batch, rows, past), F32),
                   jax.ShapeDtypeStruct((batch, rows, LANES), I32),
                   jax.ShapeDtypeStruct((batch, rows, LANES), F32)],
        compiler_params=_cparams("arbitrary", "arbitrary"),
        name="sample_scores",
    )(page_table, iq, iw, ikb_new, *([cache_ik] * pages))


def _sample_attn_kernel(pt_ref, q_ref, sc_ref, thr_ref, scn_ref, kn_ref, vn_ref, *rest, pages):
    k_refs = rest[:pages]
    v_refs = rest[pages:2 * pages]
    o_ref, qa_ref, m_ref, l_ref, acc_ref = rest[2 * pages:]
    del pt_ref
    c = pl.program_id(1)
    nchunks = pl.num_programs(1)
    rows = SUBLANES
    pcols = PAGE_SIZE * KV_HEADS
    thr = thr_ref[0][:, 0:1]

    @pl.when(c == 0)
    def _():
        for h in range(ATT_HEADS):
            qa_ref[h * rows:(h + 1) * rows, :] = q_ref[0, :, h * HEAD_DIM:(h + 1) * HEAD_DIM].astype(F32)
        m_ref[...] = jnp.full(m_ref.shape, NEG, F32)
        l_ref[...] = jnp.zeros(l_ref.shape, F32)
        acc_ref[...] = jnp.zeros(acc_ref.shape, F32)

    kk = lax.broadcasted_iota(I32, (PAGE_SIZE, KV_HEADS * pcols), 0)
    cc = lax.broadcasted_iota(I32, (PAGE_SIZE, KV_HEADS * pcols), 1)
    head_bits = KV_HEADS.bit_length() - 1
    col = cc & (pcols - 1)
    expand = jnp.where(((col >> head_bits) == kk) & ((col & (KV_HEADS - 1)) == (cc >> (pcols.bit_length() - 1))),
                       1.0, 0.0).astype(BF16)

    def update(sel, kchunk, vchunk):
        n = sel.shape[1] // PAGE_SIZE
        stack = jnp.concatenate([sel[:, p * PAGE_SIZE:(p + 1) * PAGE_SIZE] for p in range(n)], axis=0)
        e = jnp.dot(stack.astype(BF16), expand, preferred_element_type=F32)
        mask = jnp.concatenate(
            [jnp.concatenate([jnp.concatenate(
                [e[p * rows:(p + 1) * rows, g * pcols:(g + 1) * pcols] for p in range(n)], axis=1)] * GROUP, axis=0)
             for g in range(KV_HEADS)], axis=0)
        s = jnp.where(mask > 0.5, _dot_t(qa_ref[...].astype(BF16), kchunk), NEG)
        m_old = m_ref[...]
        m_new = jnp.maximum(m_old, jnp.max(s, axis=1, keepdims=True))
        alpha = jnp.exp(m_old - m_new)
        p = jnp.exp(s - m_new)
        l_ref[...] = alpha * l_ref[...] + jnp.sum(p, axis=1, keepdims=True)
        acc_ref[...] = alpha * acc_ref[...] + jnp.dot(p.astype(BF16), vchunk, preferred_element_type=F32)
        m_ref[...] = m_new

    sel = jnp.where(_sort_key(sc_ref[0]) >= thr, 1.0, 0.0)
    update(sel, jnp.concatenate([k_refs[p][0] for p in range(pages)], axis=0).astype(BF16),
           jnp.concatenate([v_refs[p][0] for p in range(pages)], axis=0).astype(BF16))

    @pl.when(c == nchunks - 1)
    def _():
        t = lax.broadcasted_iota(I32, (rows, LANES), 0)
        j = lax.broadcasted_iota(I32, (rows, LANES), 1)
        seln = jnp.where(jnp.where(j <= t, _sort_key(scn_ref[0]), INT_MIN) >= thr, 1.0, 0.0)
        update(seln, kn_ref[0].astype(BF16), vn_ref[0].astype(BF16))
        out = acc_ref[...] / l_ref[...]
        for h in range(ATT_HEADS):
            o_ref[0, :, h * HEAD_DIM:(h + 1) * HEAD_DIM] = out[h * rows:(h + 1) * rows, :].astype(BF16)


def _sample_attention(page_table, q, sc, thr, scn, kb_new, vb_new, cache_k, cache_v, pages):
    batch, n_pages = page_table.shape
    rows = SUBLANES
    nchunks = n_pages // pages
    seq = lambda b, c, pt: (b, 0, 0)
    page_map = lambda p: functools.partial(lambda b, c, pt, p: (pt[b, c * pages + p], 0, 0), p=p)
    pcols = PAGE_SIZE * KV_HEADS
    k_specs = [pl.BlockSpec((1, pcols, HEAD_DIM), page_map(p)) for p in range(pages)]
    v_specs = [pl.BlockSpec((1, pcols, HEAD_DIM), page_map(p)) for p in range(pages)]
    kern = functools.partial(_sample_attn_kernel, pages=pages)
    grid_spec = pltpu.PrefetchScalarGridSpec(
        num_scalar_prefetch=1,
        grid=(batch, nchunks),
        in_specs=[pl.BlockSpec((1, rows, ATT_Q_W), seq),
                  pl.BlockSpec((1, rows, pages * PAGE_SIZE), lambda b, c, pt: (b, 0, c)),
                  pl.BlockSpec((1, rows, LANES), seq), pl.BlockSpec((1, rows, LANES), seq),
                  pl.BlockSpec((1, pcols, HEAD_DIM), seq), pl.BlockSpec((1, pcols, HEAD_DIM), seq)]
                 + k_specs + v_specs,
        out_specs=pl.BlockSpec((1, rows, ATT_Q_W), seq),
        scratch_shapes=[pltpu.VMEM((ATT_HEADS * rows, HEAD_DIM), F32),
                        pltpu.VMEM((ATT_HEADS * rows, 1), F32),
                        pltpu.VMEM((ATT_HEADS * rows, 1), F32),
                        pltpu.VMEM((ATT_HEADS * rows, HEAD_DIM), F32)],
    )
    return pl.pallas_call(
        kern,
        grid_spec=grid_spec,
        out_shape=jax.ShapeDtypeStruct((batch, rows, ATT_Q_W), BF16),
        compiler_params=_cparams("arbitrary", "arbitrary"),
        name="sample_attention",
    )(page_table, q, sc, thr, scn, kb_new, vb_new, *([cache_k] * pages), *([cache_v] * pages))


def _ssm_kernel(u_ref, s0r_ref, s0i_ref, ar_ref, ai_ref, bre_ref, bim_ref, cre_ref, cim_ref, d_ref,
                wglu_ref, z_ref, sr_out, si_out, sre_ref, sim_ref, str_ref, sti_ref, *, tc, nb, lc, cb):
    @pl.when(pl.program_id(0) == 0)
    def _():
        str_ref[...] = s0r_ref[...]
        sti_ref[...] = s0i_ref[...]

    width = u_ref.shape[-1]
    nst = sre_ref.shape[-1]
    u = u_ref[...].reshape(tc * nb, width)
    ub = u.astype(BF16)
    sb = cb * (nst // width)
    blocks = [(slice(k * cb, (k + 1) * cb), slice(k * sb, (k + 1) * sb)) for k in range(width // cb)]
    for csl, ssl in blocks:
        sre_ref[:, ssl] = jnp.dot(ub[:, csl], bre_ref[csl, ssl], preferred_element_type=F32)
        sim_ref[:, ssl] = jnp.dot(ub[:, csl], bim_ref[csl, ssl], preferred_element_type=F32)

    for ci in range(nst // lc):
        lsl = slice(ci * lc, (ci + 1) * lc)
        ar = jnp.broadcast_to(ar_ref[:, lsl], (nb, lc))
        ai = jnp.broadcast_to(ai_ref[:, lsl], (nb, lc))

        def step(j, carry):
            sr, si = carry
            r0 = pl.multiple_of(j * nb, nb)
            nr = ar * sr - ai * si + sre_ref[pl.ds(r0, nb), lsl]
            ni = ar * si + ai * sr + sim_ref[pl.ds(r0, nb), lsl]
            sre_ref[pl.ds(r0, nb), lsl] = nr
            sim_ref[pl.ds(r0, nb), lsl] = ni
            return nr, ni
        sr, si = lax.fori_loop(0, tc, step, (str_ref[:, lsl], sti_ref[:, lsl]))
        str_ref[:, lsl] = sr
        sti_ref[:, lsl] = si

    y = jnp.concatenate(
        [jnp.dot(sre_ref[:, ssl].astype(BF16), cre_ref[ssl, csl], preferred_element_type=F32)
         - jnp.dot(sim_ref[:, ssl].astype(BF16), cim_ref[ssl, csl], preferred_element_type=F32)
         for csl, ssl in blocks], axis=1) + u * d_ref[...]
    z = _gelu_tanh(y)
    z = z * _sigmoid(jnp.dot(z.astype(BF16), wglu_ref[...], preferred_element_type=F32))
    z_ref[...] = z.reshape(tc, nb, width).astype(BF16)
    sr_out[...] = str_ref[...]
    si_out[...] = sti_ref[...]


def _ssm_branch(u_tm, s0r, s0i, ar, ai, bre, bim, cre, cim, d, wglu, tc):
    t, nb, width = u_tm.shape
    nst = ar.shape[1]
    lc = min(256, nst)
    mxu = 256
    cb = mxu if width % mxu == 0 and (mxu % SSM_GROUP == 0) else width
    kern = functools.partial(_ssm_kernel, tc=tc, nb=nb, lc=lc, cb=cb)
    return pl.pallas_call(
        kern,
        grid=(t // tc,),
        in_specs=[pl.BlockSpec((tc, nb, width), lambda c: (c, 0, 0)),
                  _const_spec((nb, nst)), _const_spec((nb, nst)),
                  _const_spec((1, nst)), _const_spec((1, nst)),
                  _const_spec((width, nst)), _const_spec((width, nst)),
                  _const_spec((nst, width)), _const_spec((nst, width)),
                  _const_spec((1, width)), _const_spec((width, width))],
        out_specs=[pl.BlockSpec((tc, nb, width), lambda c: (c, 0, 0)),
                   pl.BlockSpec((nb, nst), lambda c: (0, 0)), pl.BlockSpec((nb, nst), lambda c: (0, 0))],
        out_shape=[jax.ShapeDtypeStruct((t, nb, width), BF16),
                   jax.ShapeDtypeStruct((nb, nst), F32), jax.ShapeDtypeStruct((nb, nst), F32)],
        scratch_shapes=[pltpu.VMEM((tc * nb, nst), F32), pltpu.VMEM((tc * nb, nst), F32),
                        pltpu.VMEM((nb, nst), F32), pltpu.VMEM((nb, nst), F32)],
        compiler_params=_cparams("arbitrary"),
        name="ssm_branch",
    )(u_tm, s0r, s0i, ar, ai, bre, bim, cre, cim, d, wglu)


MERGE_CW = 512


def _merge_kernel(o_ref, z_ref, x_ref, g_ref, wg_ref, wa_ref, ws_ref, wo_ref, out_ref, mg_ref):
    x = x_ref[...]
    d = x.shape[1]
    h = _rms(x, g_ref[...]).astype(BF16)
    o = o_ref[...]
    z = z_ref[...]
    for c in range(d // MERGE_CW):
        csl = slice(c * MERGE_CW, (c + 1) * MERGE_CW)
        ga = jnp.dot(h, wg_ref[:, csl], preferred_element_type=F32)
        gs = jnp.dot(h, wg_ref[:, d + c * MERGE_CW:d + (c + 1) * MERGE_CW], preferred_element_type=F32)
        a = jnp.dot(o, wa_ref[:, csl], preferred_element_type=F32)
        s = jnp.dot(z, ws_ref[:, csl], preferred_element_type=F32)
        mg_ref[:, csl] = (_sigmoid(ga) * a + _sigmoid(gs) * s).astype(BF16)
    out_ref[...] = x + jnp.dot(mg_ref[...], wo_ref[...], preferred_element_type=F32)


def _merge(o_att, z, z_map, x, g, wg, wa, ws, wo, tm):
    n, d = x.shape
    zw = ws.shape[0]
    row = lambda i: (i, 0)
    return pl.pallas_call(
        _merge_kernel,
        grid=(n // tm,),
        in_specs=[pl.BlockSpec((tm, ATT_Q_W), row), pl.BlockSpec((tm, zw), z_map),
                  pl.BlockSpec((tm, d), row), _const_spec((1, d)), _const_spec(wg.shape),
                  _const_spec(wa.shape), _const_spec(ws.shape), _const_spec(wo.shape)],
        out_specs=pl.BlockSpec((tm, d), row),
        out_shape=jax.ShapeDtypeStruct((n, d), F32),
        scratch_shapes=[pltpu.VMEM((tm, d), BF16)],
        compiler_params=_cparams("arbitrary"),
        name="gated_merge",
    )(o_att, z, x, g, wg, wa, ws, wo)


def _shift_rows(x, head):
    return jnp.concatenate([head, x[:x.shape[0] - head.shape[0], :]], axis=0)


def _ffn_kernel(x_ref, g_ref, init_ref, wup_ref, cw_ref, cb_ref, wdn_ref, out_ref, st_ref,
                h_ref, acc_ref, hist_ref, carry_ref, *, tm, tf, shift, tiles_per_seq):
    i = pl.program_id(0)
    f = pl.program_id(1)

    @pl.when(f == 0)
    def _():
        h_ref[...] = _rms(x_ref[...], g_ref[...]).astype(BF16)
        acc_ref[...] = jnp.zeros(acc_ref.shape, F32)

    first = (i % tiles_per_seq) == 0

    @pl.when(first)
    def _():
        hist_ref[...] = init_ref[...]

    @pl.when(jnp.logical_not(first))
    def _():
        hist_ref[...] = carry_ref[f]

    def conv_piece(csl):
        up = jnp.dot(h_ref[...], wup_ref[:, csl], preferred_element_type=F32)
        hist = hist_ref[:, csl]
        prev1 = _shift_rows(up, hist[shift:, :])
        prev2 = _shift_rows(up, hist)
        conv = (cw_ref[0, 2:3, csl] * up + cw_ref[0, 1:2, csl] * prev1 + cw_ref[0, 0:1, csl] * prev2
                + cb_ref[0, :, csl])
        return conv, up[tm - 2 * shift:, :]

    down = None
    for j in range(tf // FFN_SUB):
        gsl = slice(j * FFN_SUB, (j + 1) * FFN_SUB)
        vsl = slice(tf + j * FFN_SUB, tf + (j + 1) * FFN_SUB)
        gate, gate_tail = conv_piece(gsl)
        val, val_tail = conv_piece(vsl)
        for sl, tail in ((gsl, gate_tail), (vsl, val_tail)):
            carry_ref[f, :, sl] = tail
            st_ref[0, :, sl] = tail
        act = (_gelu_tanh(gate) * val).astype(BF16)
        part = jnp.dot(act, wdn_ref[0, j * FFN_SUB:(j + 1) * FFN_SUB, :], preferred_element_type=F32)
        down = part if down is None else down + part
    acc_ref[...] += down

    @pl.when(f == pl.num_programs(1) - 1)
    def _():
        out_ref[...] = x_ref[...] + acc_ref[...]


def _conv_ffn(x, g, init, wup, cw, cb, wdn, tm, shift, tiles_per_seq):
    n, d = x.shape
    nf, tf, _ = wdn.shape
    tf2 = 2 * tf
    hist = 2 * shift
    kern = functools.partial(_ffn_kernel, tm=tm, tf=tf, shift=shift, tiles_per_seq=tiles_per_seq)
    return pl.pallas_call(
        kern,
        grid=(n // tm, nf),
        in_specs=[pl.BlockSpec((tm, d), lambda i, f: (i, 0)), _const_spec((1, d)),
                  pl.BlockSpec((hist, tf2), lambda i, f: (0, f)),
                  pl.BlockSpec((d, tf2), lambda i, f: (0, f)),
                  pl.BlockSpec((1, CONV_W, tf2), lambda i, f: (f, 0, 0)),
                  pl.BlockSpec((1, 1, tf2), lambda i, f: (f, 0, 0)),
                  pl.BlockSpec((1, tf, d), lambda i, f: (f, 0, 0))],
        out_specs=[pl.BlockSpec((tm, d), lambda i, f: (i, 0)),
                   pl.BlockSpec((1, hist, tf2), lambda i, f: (i, 0, f))],
        out_shape=[jax.ShapeDtypeStruct((n, d), F32),
                   jax.ShapeDtypeStruct((n // tm, hist, nf * tf2), F32)],
        scratch_shapes=[pltpu.VMEM((tm, d), BF16), pltpu.VMEM((tm, d), F32),
                        pltpu.VMEM((hist, tf2), F32), pltpu.VMEM((nf, hist, tf2), F32)],
        compiler_params=_cparams("arbitrary", "arbitrary"),
        name="conv_ffn",
    )(x, g, init, wup, cw, cb, wdn)


def _ple_kernel(x_ref, p_ref, gpg_ref, wpg_ref, wple_ref, gple_ref, out_ref):
    x = x_ref[...]
    ple = _rms(jnp.dot(p_ref[...].astype(BF16), wple_ref[...], preferred_element_type=F32), gple_ref[...])
    gate = _sigmoid(jnp.dot(_rms(x, gpg_ref[...]).astype(BF16), wpg_ref[...], preferred_element_type=F32))
    out_ref[...] = x + gate * ple


def _ple(x, p, gpg, wpg, wple, gple, tm):
    n, d = x.shape
    pd = p.shape[1]
    row = lambda i: (i, 0)
    return pl.pallas_call(
        _ple_kernel,
        grid=(n // tm,),
        in_specs=[pl.BlockSpec((tm, d), row), pl.BlockSpec((tm, pd), row), _const_spec((1, d)),
                  _const_spec((d, d)), _const_spec((pd, d)), _const_spec((1, d))],
        out_specs=pl.BlockSpec((tm, d), row),
        out_shape=jax.ShapeDtypeStruct((n, d), F32),
        compiler_params=_cparams("arbitrary"),
        name="ple_gate",
    )(x, p, gpg, wpg, wple, gple)


def _rope_tables(pos, head_dim, rot_dim):
    half = rot_dim // 2
    inv_freq = jnp.asarray(
        np.float32(ROPE_THETA) ** (-np.arange(half, dtype=np.float32) / np.float32(half)))
    ang = pos.astype(F32)[:, None] * inv_freq[None, :]
    cos, sin = jnp.cos(ang), jnp.sin(ang)
    n = pos.shape[0]
    ones = jnp.ones((n, head_dim - rot_dim), F32)
    zeros = jnp.zeros((n, head_dim - rot_dim), F32)
    zh = jnp.zeros((n, half), F32)
    c = jnp.concatenate([cos, cos, ones], axis=1)
    sa = jnp.concatenate([zh, sin, zeros], axis=1)
    sb = jnp.concatenate([-sin, zh, zeros], axis=1)
    reps = LANES // head_dim
    return jnp.stack([jnp.tile(t, (1, reps)) for t in (c, sa, sb)], axis=0)


def _ssm_params(lw):
    a_re = lw['ssm_a_re'].astype(F32)
    a_im = lw['ssm_a_im'].astype(F32)
    groups, nstate = a_re.shape
    dt = jnp.exp(lw['ssm_log_dt'].astype(F32))[:, None]
    mag = jnp.exp(a_re * dt)
    ang = a_im * dt
    ab_re, ab_im = mag * jnp.cos(ang), mag * jnp.sin(ang)
    den = a_re * a_re + a_im * a_im
    f_re = ((ab_re - 1.0) * a_re + ab_im * a_im) / den
    f_im = (ab_im * a_re - (ab_re - 1.0) * a_im) / den
    b_re = lw['ssm_b_re'].astype(F32)
    b_im = lw['ssm_b_im'].astype(F32)
    bb_re = f_re[..., None] * b_re - f_im[..., None] * b_im
    bb_im = f_re[..., None] * b_im + f_im[..., None] * b_re
    eye = jnp.eye(groups, dtype=F32)

    def in_map(bb):
        m = jnp.einsum('gnc,gh->gchn', bb, eye)
        return m.reshape(groups * SSM_GROUP, groups * nstate).astype(BF16)

    def out_map(cc):
        m = jnp.einsum('gcn,gh->gnhc', cc.astype(F32), eye)
        return m.reshape(groups * nstate, groups * SSM_GROUP).astype(BF16)

    return dict(ar=ab_re.reshape(1, -1), ai=ab_im.reshape(1, -1),
                bre=in_map(bb_re), bim=in_map(bb_im),
                cre=out_map(lw['ssm_c_re']), cim=out_map(lw['ssm_c_im']),
                d=lw['ssm_d'].astype(F32).reshape(1, -1), wglu=lw['w_glu'].astype(BF16))


def _chunk_features(a, d_ff, tf):
    nf = -(-d_ff // tf)
    pad = [(0, 0)] * (a.ndim - 1) + [(0, nf * tf - d_ff)]
    gate = jnp.pad(a[..., :d_ff], pad).reshape(a.shape[:-1] + (nf, tf))
    val = jnp.pad(a[..., d_ff:], pad).reshape(a.shape[:-1] + (nf, tf))
    return jnp.concatenate([gate, val], axis=-1)


def _unchunk_features(a, d_ff, tf):
    nf = a.shape[-1] // (2 * tf)
    a = a.reshape(a.shape[:-1] + (nf, 2, tf))
    gate = a[..., 0, :].reshape(a.shape[:-3] + (nf * tf,))[..., :d_ff]
    val = a[..., 1, :].reshape(a.shape[:-3] + (nf * tf,))[..., :d_ff]
    return jnp.concatenate([gate, val], axis=-1)


FFN_TF = 512
FFN_SUB = 256


def _layer_weights(lw):
    w_in = lw['w_in']
    d = w_in.shape[0]
    o = np.cumsum([0, ATT_Q_W, ATT_KV_W, ATT_KV_W, IDX_Q_W, IDX_DIM, IDX_HEADS])
    ssm_w = lw['ssm_d'].shape[0]
    w_q_kv = w_in[:, :o[3]]
    w_iq, w_ik, w_iw = w_in[:, o[3]:o[4]], w_in[:, o[4]:o[5]], w_in[:, o[5]:o[6]]
    w_u = w_in[:, o[6]:o[6] + ssm_w]
    w_gates = w_in[:, o[6] + ssm_w:]
    w_idx = jnp.concatenate([w_iq, w_ik, w_ik, w_iw, jnp.zeros((d, LANES - IDX_HEADS), w_in.dtype)], axis=1)
    d_ff = lw['w_down'].shape[0]
    nf = -(-d_ff // FFN_TF)
    wup = _chunk_features(lw['w_up'], d_ff, FFN_TF).reshape(d, nf * 2 * FFN_TF).astype(BF16)
    wdn = jnp.pad(lw['w_down'], ((0, nf * FFN_TF - d_ff), (0, 0))).reshape(nf, FFN_TF, d).astype(BF16)
    cw = jnp.transpose(_chunk_features(lw['conv_w'], d_ff, FFN_TF), (1, 0, 2)).astype(F32)
    cb = _chunk_features(lw['conv_b'], d_ff, FFN_TF).reshape(nf, 1, 2 * FFN_TF).astype(F32)
    out = dict(
        g_mix=lw['g_mix'].reshape(1, d), g_q=lw['g_q'].reshape(1, HEAD_DIM), g_k=lw['g_k'].reshape(1, HEAD_DIM),
        w_qkv=w_q_kv.astype(BF16), w_idx=w_idx.astype(BF16), w_u=w_u.astype(BF16),
        w_gates=w_gates.astype(BF16),
        w_att_br=lw['w_att_br'].astype(BF16), w_ssm_br=lw['w_ssm_br'].astype(BF16), w_o=lw['w_o'].astype(BF16),
        g_ffn=lw['g_ffn'].reshape(1, d), wup=wup, wdn=wdn, cw=cw, cb=cb, d_ff=d_ff,
        g_pg=lw['g_pg'].reshape(1, d), w_pg=lw['w_pg'].astype(BF16), w_ple=lw['w_ple'].astype(BF16),
        g_ple=lw['g_ple'].reshape(1, d))
    out.update(_ssm_params(lw))
    return out


def _row_tile(n, cap):
    return cap if n % cap == 0 else n


def _prompt_layer(x3, p3, w):
    batch, seq, d = x3.shape
    n = batch * seq
    x = x3.reshape(n, d)
    tm = _row_tile(seq, 512)
    tpb = seq // tm
    pos = jnp.arange(seq)
    tab_att = _rope_tables(pos, HEAD_DIM, ROT_DIM)
    tab_idx = _rope_tables(pos, IDX_DIM, IDX_ROT_DIM)

    ssm_w = w['w_u'].shape[1]
    q, k, v, kb, vb, iq, ik2, ikb, iw, u_tm = _in_proj(
        x, w['g_mix'], w['w_qkv'], w['w_idx'], w['w_u'], w['g_q'], w['g_k'], tab_att, tab_idx, tm,
        (seq, batch * ssm_w), lambda i: (i % tpb, i // tpb))
    u_tm = u_tm.reshape(seq, batch, ssm_w)

    o_att = _prompt_attention(q, iq, iw, ikb, kb, vb, batch, seq)

    nst = w['ar'].shape[1]
    zeros = jnp.zeros((batch, nst), F32)
    z_tm, s_re, s_im = _ssm_branch(u_tm, zeros, zeros, w['ar'], w['ai'], w['bre'], w['bim'],
                                   w['cre'], w['cim'], w['d'], w['wglu'], tc=min(32, seq))
    tmm = _row_tile(seq, 256)
    tpbm = seq // tmm
    x1 = _merge(o_att, z_tm.reshape(seq, batch * ssm_w), lambda i: (i % tpbm, i // tpbm),
                x, w['g_mix'], w['w_gates'], w['w_att_br'], w['w_ssm_br'], w['w_o'], tmm)

    nf, tf2 = w['wdn'].shape[0], 2 * w['wdn'].shape[1]
    x2, conv = _conv_ffn(x1, w['g_ffn'], jnp.zeros((CONV_W - 1, nf * tf2), F32),
                         w['wup'], w['cw'], w['cb'], w['wdn'], tm, 1, tpb)
    conv = _unchunk_features(conv[tpb - 1::tpb], w['d_ff'], tf2 // 2)
    y = _ple(x2, p3.reshape(n, -1), w['g_pg'], w['w_pg'], w['w_ple'], w['g_ple'], tm)

    groups = nst // SSM_STATE
    return (y.reshape(batch, seq, d), k.reshape(batch, seq, KV_HEADS, HEAD_DIM),
            v.reshape(batch, seq, KV_HEADS, HEAD_DIM), ik2[:, :IDX_DIM].reshape(batch, seq, IDX_DIM),
            s_re.reshape(batch, groups, SSM_STATE), s_im.reshape(batch, groups, SSM_STATE), conv)


def _pad_rows(a, rows):
    pad = [(0, 0)] * a.ndim
    pad[1] = (0, rows - a.shape[1])
    return jnp.pad(a, pad)


def _sample_layer(x3, p3, cache_k, cache_v, cache_ik, s0_re, s0_im, conv0, page_table, w):
    batch, t_new, d = x3.shape
    n = batch * t_new
    n_pages = page_table.shape[1]
    past = n_pages * PAGE_SIZE
    x = x3.reshape(n, d)
    pos = past + (jnp.arange(n) % t_new)
    tab_att = _rope_tables(pos, HEAD_DIM, ROT_DIM)
    tab_idx = _rope_tables(pos, IDX_DIM, IDX_ROT_DIM)

    ssm_w = w['w_u'].shape[1]
    q, k, v, kb, vb, iq, ik2, ikb, iw, u = _in_proj(
        x, w['g_mix'], w['w_qkv'], w['w_idx'], w['w_u'], w['g_q'], w['g_k'], tab_att, tab_idx, n,
        (n, ssm_w), lambda i: (i, 0))

    per_seq = lambda a: a.reshape(batch, t_new, a.shape[-1])
    pages = max(p for p in (32, 16, 8, 4, 2, 1) if n_pages % p == 0)
    sc, thr, scn = _sample_scores(
        page_table, _pad_rows(per_seq(iq), SUBLANES), _pad_rows(per_seq(iw), SUBLANES),
        jnp.swapaxes(_pad_rows(per_seq(ikb[:, :IDX_DIM]), LANES), 1, 2), cache_ik, pages, t_new)
    new_page = lambda a: _pad_rows(a.reshape(batch, t_new * KV_HEADS, HEAD_DIM), PAGE_SIZE * KV_HEADS)
    o_att = _sample_attention(
        page_table, _pad_rows(per_seq(q), SUBLANES), sc, thr, scn,
        new_page(k), new_page(v), cache_k, cache_v, pages)
    o_att = o_att[:, :t_new].reshape(n, ATT_Q_W)

    nst = w['ar'].shape[1]
    u_tm = jnp.transpose(u.reshape(batch, t_new, ssm_w), (1, 0, 2))
    z_tm, s_re, s_im = _ssm_branch(u_tm, s0_re.reshape(batch, nst).astype(F32),
                                   s0_im.reshape(batch, nst).astype(F32), w['ar'], w['ai'], w['bre'],
                                   w['bim'], w['cre'], w['cim'], w['d'], w['wglu'], tc=t_new)
    z = jnp.transpose(z_tm, (1, 0, 2)).reshape(n, ssm_w)
    x1 = _merge(o_att, z, lambda i: (i, 0), x, w['g_mix'], w['w_gates'], w['w_att_br'], w['w_ssm_br'],
                w['w_o'], n)

    nf, tf2 = w['wdn'].shape[0], 2 * w['wdn'].shape[1]
    d_ff = w['d_ff']
    x1_tm = jnp.transpose(x1.reshape(batch, t_new, d), (1, 0, 2)).reshape(n, d)
    init = _chunk_features(jnp.transpose(conv0.astype(F32), (1, 0, 2)), d_ff, tf2 // 2)
    init = init.reshape((CONV_W - 1) * batch, nf * tf2)
    x2_tm, conv = _conv_ffn(x1_tm, w['g_ffn'], init, w['wup'], w['cw'], w['cb'], w['wdn'], n, batch, 1)
    conv = _unchunk_features(conv.reshape(CONV_W - 1, batch, nf * tf2), d_ff, tf2 // 2)
    conv = jnp.transpose(conv, (1, 0, 2))
    x2 = jnp.transpose(x2_tm.reshape(t_new, batch, d), (1, 0, 2)).reshape(n, d)
    y = _ple(x2, p3.reshape(n, -1), w['g_pg'], w['w_pg'], w['w_ple'], w['g_ple'], n)

    groups = nst // SSM_STATE
    return (y.reshape(batch, t_new, d), k.reshape(batch, t_new, KV_HEADS, HEAD_DIM),
            v.reshape(batch, t_new, KV_HEADS, HEAD_DIM), ik2[:, :IDX_DIM].reshape(batch, t_new, IDX_DIM),
            s_re.reshape(batch, groups, SSM_STATE), s_im.reshape(batch, groups, SSM_STATE), conv)


def kernel(x_prompt, x_sample, cache_k, cache_v, cache_idx_k, state_ssm_re, state_ssm_im, state_conv, page_table, p_prompt, p_sample, g_mix, w_in, g_q, g_k, ssm_a_re, ssm_a_im, ssm_log_dt, ssm_b_re, ssm_b_im, ssm_c_re, ssm_c_im, ssm_d, w_glu, w_att_br, w_ssm_br, w_o, g_ffn, w_up, conv_w, conv_b, w_down, w_ple, g_ple, g_pg, w_pg):
    depth = w_in.shape[0]
    n_pool = cache_k.shape[1]
    pool_k = cache_k.reshape(depth * n_pool, PAGE_SIZE * KV_HEADS, HEAD_DIM)
    pool_v = cache_v.reshape(depth * n_pool, PAGE_SIZE * KV_HEADS, HEAD_DIM)
    pool_ik = jnp.swapaxes(cache_idx_k, 2, 3).reshape(depth * n_pool, IDX_DIM, PAGE_SIZE)
    xp, xs = x_prompt, x_sample
    outs_p, outs_s = [], []
    for i in range(depth):
        lw = dict(g_mix=g_mix[i], w_in=w_in[i], g_q=g_q[i], g_k=g_k[i],
                  ssm_a_re=ssm_a_re[i], ssm_a_im=ssm_a_im[i], ssm_log_dt=ssm_log_dt[i],
                  ssm_b_re=ssm_b_re[i], ssm_b_im=ssm_b_im[i], ssm_c_re=ssm_c_re[i], ssm_c_im=ssm_c_im[i],
                  ssm_d=ssm_d[i], w_glu=w_glu[i], w_att_br=w_att_br[i], w_ssm_br=w_ssm_br[i], w_o=w_o[i],
                  g_ffn=g_ffn[i], w_up=w_up[i], conv_w=conv_w[i], conv_b=conv_b[i], w_down=w_down[i],
                  w_ple=w_ple[i], g_ple=g_ple[i], g_pg=g_pg[i], w_pg=w_pg[i])
        w = _layer_weights(lw)
        xp, *rest_p = _prompt_layer(xp, p_prompt[i], w)
        outs_p.append(rest_p)
        xs, *rest_s = _sample_layer(xs, p_sample[i], pool_k, pool_v, pool_ik, state_ssm_re[i],
                                    state_ssm_im[i], state_conv[i], page_table + i * n_pool, w)
        outs_s.append(rest_s)
    stack = lambda outs, j: jnp.stack([o[j] for o in outs], 0)
    return ((xp, xs) + tuple(stack(outs_p, j) for j in range(6))
            + tuple(stack(outs_s, j) for j in range(6)))
```
